```python
import jax, jax.numpy as jnp
from jax import lax
import numpy as np

D_MODEL = 1024
BATCH = 8
SEQ = 4096
DEPTH = 2

N_MEM = 256
DA_HEADS = 4
DA_HEAD_DIM = 64
DA_V_DIM = 2 * DA_HEAD_DIM
DA_WIDTH = DA_HEADS * DA_V_DIM
LRU_WIDTH = 512
LRU_BLOCKS = 8
LRU_BLOCK = LRU_WIDTH // LRU_BLOCKS
LRU_C = 8.0
CONV_WIDTH = 4
CONV_LEFT = 2
CONV_RIGHT = 1
MIX_WIDTH = DA_WIDTH + LRU_WIDTH
IN_WIDTH = 3 * DA_WIDTH + 2 * LRU_WIDTH
ROPE_THETA = 500000.0
ROPE_DIM = DA_HEAD_DIM // 4
X_HEADS = 4
X_HEAD_DIM = D_MODEL // X_HEADS
D_FF = 4 * D_MODEL
Q_BLOCK = 128
EPS = 1e-6

kernel_name = "hybrid_diffattn_rglru_memory_encoder"


def rms_norm(x, g):
    xf = x.astype(jnp.float32)
    y = xf * lax.rsqrt(jnp.mean(xf * xf, axis=-1, keepdims=True) + EPS)
    return (y * g.astype(jnp.float32)).astype(x.dtype)


def rope_tables(positions):
    inv_freq = jnp.power(jnp.float32(ROPE_THETA),
                         -jnp.arange(0, ROPE_DIM, 2, dtype=jnp.float32) / ROPE_DIM)
    ang = positions.astype(jnp.float32)[..., None] * inv_freq
    return jnp.cos(ang)[:, :, None, None, :], jnp.sin(ang)[:, :, None, None, :]


def partial_rope(t, cos, sin):
    half = ROPE_DIM // 2
    c = cos.astype(t.dtype)
    s = sin.astype(t.dtype)
    t1 = t[..., :half]
    t2 = t[..., half:ROPE_DIM]
    return jnp.concatenate([t1 * c - t2 * s, t2 * c + t1 * s, t[..., ROPE_DIM:]], axis=-1)


def diff_attention(q, k, v, lam):
    B, S = q.shape[0], q.shape[1]
    nb = S // Q_BLOCK
    scale = DA_HEAD_DIM ** -0.5
    k1 = k[:, :, :, 0]
    k2 = k[:, :, :, 1]
    qb = q.reshape(B, nb, Q_BLOCK, DA_HEADS, 2, DA_HEAD_DIM).transpose(1, 0, 2, 3, 4, 5)

    def block(qblk):
        s1 = jnp.einsum('bqhd,bkhd->bhqk', qblk[:, :, :, 0], k1,
                        preferred_element_type=jnp.float32) * scale
        s2 = jnp.einsum('bqhd,bkhd->bhqk', qblk[:, :, :, 1], k2,
                        preferred_element_type=jnp.float32) * scale
        p = jax.nn.softmax(s1, axis=-1) - lam * jax.nn.softmax(s2, axis=-1)
        return jnp.einsum('bhqk,bkhe->bqhe', p.astype(v.dtype), v)

    o = lax.map(block, qb)
    return o.transpose(1, 0, 2, 3, 4).reshape(B, S, DA_HEADS, DA_V_DIM)


def centred_dwconv(x, w, b):
    S = x.shape[1]
    xp = jnp.pad(x, ((0, 0), (CONV_LEFT, CONV_RIGHT), (0, 0)))
    y = b
    for j in range(CONV_WIDTH):
        y = y + xp[:, j:j + S] * w[j]
    return y


def block_diag(x, w, b):
    B, S = x.shape[0], x.shape[1]
    xb = x.reshape(B, S, LRU_BLOCKS, LRU_BLOCK)
    return (jnp.einsum('bsni,nij->bsnj', xb, w) + b).reshape(B, S, LRU_WIDTH)


def _linrec(e1, e2):
    a1, b1 = e1
    a2, b2 = e2
    return a1 * a2, a2 * b1 + b2


def rg_lru(x, w_r, b_r, w_i, b_i, a_param, reverse):
    r = jax.nn.sigmoid(block_diag(x, w_r, b_r).astype(jnp.float32))
    i = jax.nn.sigmoid(block_diag(x, w_i, b_i).astype(jnp.float32))
    log_a = -LRU_C * r * jax.nn.softplus(-a_param.astype(jnp.float32))
    a = jnp.exp(log_a)
    u = x.astype(jnp.float32) * i * jnp.sqrt(-jnp.expm1(2.0 * log_a))
    _, h = lax.associative_scan(_linrec, (a, u), axis=1, reverse=reverse)
    return h.astype(x.dtype)


def setup_inputs(seed: int = 0) -> dict:
    key = jax.random.key(seed)
    ks = iter(jax.random.split(key, 64))
    f32 = jnp.float32

    def nrm(shape, scale):
        return jax.random.normal(next(ks), shape, f32) * scale

    def gain(shape):
        return 1.0 + 0.02 * jax.random.normal(next(ks), shape, f32)

    u = jax.random.uniform(next(ks), (DEPTH, 2, LRU_WIDTH), f32, 0.9, 0.999)
    a0 = u ** (1.0 / LRU_C)
    lru_a_param = jnp.log(a0) - jnp.log1p(-a0)

    offs = jax.random.randint(next(ks), (BATCH, 1), 0, 1024, dtype=jnp.int32)
    positions = jnp.arange(SEQ, dtype=jnp.int32)[None, :] + offs

    return {
        "x": nrm((BATCH, SEQ, D_MODEL), 1.0),
        "mem": nrm((BATCH, N_MEM, D_MODEL), 1.0),
        "positions": positions,
        "g_mix": gain((DEPTH, D_MODEL)),
        "w_in": nrm((DEPTH, D_MODEL, IN_WIDTH), D_MODEL ** -0.5),
        "da_lq1": nrm((DEPTH, DA_HEAD_DIM), 0.1),
        "da_lk1": nrm((DEPTH, DA_HEAD_DIM), 0.1),
        "da_lq2": nrm((DEPTH, DA_HEAD_DIM), 0.1),
        "da_lk2": nrm((DEPTH, DA_HEAD_DIM), 0.1),
        "da_subln_g": gain((DEPTH, DA_V_DIM)),
        "lru_conv_w": nrm((DEPTH, CONV_WIDTH, LRU_WIDTH), CONV_WIDTH ** -0.5),
        "lru_conv_b": nrm((DEPTH, LRU_WIDTH), 0.01),
        "lru_w_r": nrm((DEPTH, 2, LRU_BLOCKS, LRU_BLOCK, LRU_BLOCK), LRU_BLOCK ** -0.5),
        "lru_b_r": nrm((DEPTH, 2, LRU_BLOCKS, LRU_BLOCK), 0.01),
        "lru_w_i": nrm((DEPTH, 2, LRU_BLOCKS, LRU_BLOCK, LRU_BLOCK), LRU_BLOCK ** -0.5),
        "lru_b_i": nrm((DEPTH, 2, LRU_BLOCKS, LRU_BLOCK), 0.01),
        "lru_a_param": lru_a_param,
        "lru_norm_g": gain((DEPTH, LRU_WIDTH)),
        "w_out": nrm((DEPTH, MIX_WIDTH, D_MODEL), MIX_WIDTH ** -0.5),
        "g_xq": gain((DEPTH, D_MODEL)),
        "g_mem": gain((DEPTH, D_MODEL)),
        "xa_wq": nrm((DEPTH, D_MODEL, D_MODEL), D_MODEL ** -0.5),
        "xa_wk": nrm((DEPTH, D_MODEL, D_MODEL), D_MODEL ** -0.5),
        "xa_wv": nrm((DEPTH, D_MODEL, D_MODEL), D_MODEL ** -0.5),
        "xa_wo": nrm((DEPTH, D_MODEL, D_MODEL), D_MODEL ** -0.5),
        "g_mlp": gain((DEPTH, D_MODEL)),
        "mlp_w1": nrm((DEPTH, D_MODEL, D_FF), D_MODEL ** -0.5),
        "mlp_w2": nrm((DEPTH, D_FF, D_MODEL), D_FF ** -0.5),
        "g_final": gain((D_MODEL,)),
    }


def reference(x, mem, positions, g_mix, w_in, da_lq1, da_lk1, da_lq2, da_lk2, da_subln_g,
              lru_conv_w, lru_conv_b, lru_w_r, lru_b_r, lru_w_i, lru_b_i, lru_a_param,
              lru_norm_g, w_out, g_xq, g_mem, xa_wq, xa_wk, xa_wv, xa_wo,
              g_mlp, mlp_w1, mlp_w2, g_final):
    B, S = x.shape[0], x.shape[1]
    M = mem.shape[1]
    cos, sin = rope_tables(positions)

    for l in range(DEPTH):
        h = rms_norm(x, g_mix[l])
        z = h @ w_in[l]
        q = z[..., :DA_WIDTH].reshape(B, S, DA_HEADS, 2, DA_HEAD_DIM)
        k = z[..., DA_WIDTH:2 * DA_WIDTH].reshape(B, S, DA_HEADS, 2, DA_HEAD_DIM)
        v = z[..., 2 * DA_WIDTH:3 * DA_WIDTH].reshape(B, S, DA_HEADS, DA_V_DIM)
        xb = z[..., 3 * DA_WIDTH:3 * DA_WIDTH + LRU_WIDTH]
        gb = z[..., 3 * DA_WIDTH + LRU_WIDTH:]

        q = partial_rope(q, cos, sin)
        k = partial_rope(k, cos, sin)
        lam_init = 0.8 - 0.6 * float(np.exp(-0.3 * l))
        lam = (jnp.exp(jnp.sum(da_lq1[l].astype(jnp.float32) * da_lk1[l].astype(jnp.float32)))
               - jnp.exp(jnp.sum(da_lq2[l].astype(jnp.float32) * da_lk2[l].astype(jnp.float32)))
               + lam_init)
        o = diff_attention(q, k, v, lam)
        o = (rms_norm(o, da_subln_g[l]) * (1.0 - lam_init)).reshape(B, S, DA_WIDTH)

        c = centred_dwconv(xb, lru_conv_w[l], lru_conv_b[l])
        h_f = rg_lru(c, lru_w_r[l, 0], lru_b_r[l, 0], lru_w_i[l, 0], lru_b_i[l, 0],
                     lru_a_param[l, 0], False)
        h_b = rg_lru(c, lru_w_r[l, 1], lru_b_r[l, 1], lru_w_i[l, 1], lru_b_i[l, 1],
                     lru_a_param[l, 1], True)
        r = rms_norm((h_f + h_b) * jax.nn.gelu(gb), lru_norm_g[l])

        x = x + jnp.concatenate([o, r], axis=-1) @ w_out[l]

        hq = rms_norm(x, g_xq[l])
        m = rms_norm(mem, g_mem[l])
        qx = (hq @ xa_wq[l]).reshape(B, S, X_HEADS, X_HEAD_DIM)
        kx = (m @ xa_wk[l]).reshape(B, M, X_HEADS, X_HEAD_DIM)
        vx = (m @ xa_wv[l]).reshape(B, M, X_HEADS, X_HEAD_DIM)
        s = jnp.einsum('bshd,bmhd->bhsm', qx, kx,
                       preferred_element_type=jnp.float32) * (X_HEAD_DIM ** -0.5)
        p = jax.nn.softmax(s, axis=-1).astype(vx.dtype)
        ox = jnp.einsum('bhsm,bmhd->bshd', p, vx).reshape(B, S, D_MODEL)
        x = x + ox @ xa_wo[l]

        hm = rms_norm(x, g_mlp[l])
        x = x + jnp.square(jax.nn.relu(hm @ mlp_w1[l])) @ mlp_w2[l]

    return rms_norm(x, g_final)
```

```python
import functools
import math

import numpy as np
import jax
import jax.numpy as jnp
from jax import lax
from jax.experimental import pallas as pl
from jax.experimental.pallas import tpu as pltpu

F32 = jnp.float32
BF16 = jnp.bfloat16

D_MODEL = 1024
N_MEM = 256
DA_HEADS = 4
DA_HEAD_DIM = 64
DA_V_DIM = 128
DA_WIDTH = 512
LRU_WIDTH = 512
LRU_BLOCKS = 8
LRU_BLOCK = 64
LRU_C = 8.0
ROPE_THETA = 500000.0
ROPE_DIM = 16
X_HEADS = 4
X_HEAD_DIM = 256
D_FF = 4096
EPS = 1e-6

LANES = 128
SUBLANES = 8
MIB = 1024 * 1024

ROW_TILE = 512
ATTN_Q_TILE = 256
LRU_CHUNK = 512


def _rms(x, g):
    return x * lax.rsqrt(jnp.mean(x * x, axis=-1, keepdims=True) + EPS) * g


def _const_spec(shape):
    nd = len(shape)
    return pl.BlockSpec(shape, lambda *_: (0,) * nd, pipeline_mode=pl.Buffered(1))


def _params(sem, vmem_mib):
    return pltpu.CompilerParams(dimension_semantics=sem, vmem_limit_bytes=vmem_mib * MIB)


def _in_proj_kernel(x_ref, g_ref, w_ref, rc_ref, rs1_ref, rs2_ref,
                    q_ref, k_ref, v_ref, xb_ref, gb_ref):
    h = _rms(x_ref[...], g_ref[...]).astype(BF16)
    rc = rc_ref[...]
    rs1 = rs1_ref[...]
    rs2 = rs2_ref[...]

    def rope(t):
        return (t * rc + pltpu.roll(t, LANES - ROPE_DIM // 2, 1) * rs1
                + pltpu.roll(t, ROPE_DIM // 2, 1) * rs2)

    zq = jnp.dot(h, w_ref[:, 0:DA_WIDTH], preferred_element_type=F32)
    zk = jnp.dot(h, w_ref[:, DA_WIDTH:2 * DA_WIDTH], preferred_element_type=F32)
    scale = DA_HEAD_DIM ** -0.5
    for j in range(DA_WIDTH // LANES):
        sl = slice(j * LANES, (j + 1) * LANES)
        q_ref[:, sl] = (rope(zq[:, sl]) * scale).astype(BF16)
        k_ref[:, sl] = rope(zk[:, sl]).astype(BF16)
    v_ref[...] = jnp.dot(h, w_ref[:, 2 * DA_WIDTH:3 * DA_WIDTH],
                         preferred_element_type=F32).astype(BF16)
    xb_ref[...] = jnp.dot(h, w_ref[:, 3 * DA_WIDTH:3 * DA_WIDTH + LRU_WIDTH],
                          preferred_element_type=F32)
    gb_ref[...] = jnp.dot(h, w_ref[:, 3 * DA_WIDTH + LRU_WIDTH:],
                          preferred_element_type=F32)


def _in_proj(x2, g, w, rc, rs1, rs2):
    T = x2.shape[0]
    tm = ROW_TILE
    row = lambda n: pl.BlockSpec((tm, n), lambda i: (i, 0))
    return pl.pallas_call(
        _in_proj_kernel,
        grid=(T // tm,),
        in_specs=[row(D_MODEL), _const_spec((1, D_MODEL)), _const_spec(w.shape),
                  row(LANES), row(LANES), row(LANES)],
        out_specs=[row(DA_WIDTH)] * 5,
        out_shape=[jax.ShapeDtypeStruct((T, DA_WIDTH), BF16)] * 3
        + [jax.ShapeDtypeStruct((T, LRU_WIDTH), F32)] * 2,
        compiler_params=_params(("parallel",), 40),
        name="in_proj",
    )(x2, g, w, rc, rs1, rs2)


def _attn_kernel(lq1_ref, lk1_ref, lq2_ref, lk2_ref, g_ref, q_ref, k_ref, v_ref, o_ref,
                 *, lam_init):
    lam = (jnp.exp(jnp.sum(lq1_ref[...] * lk1_ref[...], axis=-1, keepdims=True))
           - jnp.exp(jnp.sum(lq2_ref[...] * lk2_ref[...], axis=-1, keepdims=True))
           + lam_init)
    q = q_ref[...]
    lane = lax.broadcasted_iota(jnp.int32, q.shape, 1)
    zero = jnp.zeros_like(q)
    k = k_ref[...]
    nt = (((1,), (1,)), ((), ()))

    def softmax_part(qm):
        s = lax.dot_general(qm, k, nt, preferred_element_type=F32)
        e = jnp.exp(s - jnp.max(s, axis=-1, keepdims=True))
        return e, jnp.sum(e, axis=-1, keepdims=True)

    e1, l1 = softmax_part(jnp.where(lane < DA_HEAD_DIM, q, zero))
    e2, l2 = softmax_part(jnp.where(lane >= DA_HEAD_DIM, q, zero))
    p = e1 * (1.0 / l1) - e2 * (lam / l2)
    o = jnp.dot(p.astype(BF16), v_ref[...], preferred_element_type=F32)
    o_ref[...] = (_rms(o, g_ref[...]) * (1.0 - lam_init)).astype(BF16)


def _attention(q, k, v, lq1, lk1, lq2, lk2, g, lam_init, B, S):
    T = q.shape[0]
    tq = ATTN_Q_TILE
    nq = S // tq
    vec = _const_spec((1, DA_HEAD_DIM))
    qspec = pl.BlockSpec((tq, LANES), lambda b, h, i: (b * nq + i, h))
    kvspec = pl.BlockSpec((S, LANES), lambda b, h, i: (b, h))
    return pl.pallas_call(
        functools.partial(_attn_kernel, lam_init=lam_init),
        grid=(B, DA_HEADS, nq),
        in_specs=[vec, vec, vec, vec, _const_spec((1, DA_V_DIM)), qspec, kvspec, kvspec],
        out_specs=qspec,
        out_shape=jax.ShapeDtypeStruct((T, DA_WIDTH), BF16),
        compiler_params=_params(("parallel", "parallel", "arbitrary"), 48),
        name="diff_attn",
    )(lq1, lk1, lq2, lk2, g, q, k, v)


def _lru_kernel(xp_ref, xc_ref, xn_ref, gb_ref, cw_ref, cb_ref, wg_ref, bg_ref, ap_ref,
                ng_ref, o_ref, xe_ref, a_ref, u_ref, h_ref, hf_ref, carry_ref, *, nc, tc):
    p = pl.program_id(1)
    j = pl.program_id(2)
    chunk = j + p * (nc - 1 - 2 * j)

    xe_ref[0:SUBLANES, :] = jnp.where(chunk == 0, 0.0, xp_ref[...])
    xe_ref[SUBLANES:SUBLANES + tc, :] = xc_ref[...]
    xe_ref[SUBLANES + tc:2 * SUBLANES + tc, :] = jnp.where(chunk == nc - 1, 0.0, xn_ref[...])
    cw = cw_ref[...]
    c = cb_ref[...]
    for tap in range(4):
        c = c + xe_ref[SUBLANES - 2 + tap:SUBLANES - 2 + tap + tc, :] * cw[tap:tap + 1, :]

    half = LRU_WIDTH // 2
    c16 = c.astype(BF16)
    g_lo = jnp.dot(c16[:, :half], wg_ref[0, 0], preferred_element_type=F32)
    g_hi = jnp.dot(c16[:, half:], wg_ref[0, 1], preferred_element_type=F32)
    bg = bg_ref[0]
    r = jax.nn.sigmoid(jnp.concatenate([g_lo[:, :half], g_hi[:, :half]], axis=1)
                       + bg[:, :LRU_WIDTH])
    i = jax.nn.sigmoid(jnp.concatenate([g_lo[:, half:], g_hi[:, half:]], axis=1)
                       + bg[:, LRU_WIDTH:])
    nap = -ap_ref[0]
    softplus = jnp.maximum(nap, 0.0) + jnp.log1p(jnp.exp(-jnp.abs(nap)))
    a = jnp.exp(-LRU_C * r * softplus)
    a_ref[...] = a
    u_ref[...] = c * i * jnp.sqrt(jnp.maximum(1.0 - a * a, 0.0))

    row = lax.broadcasted_iota(jnp.int32, (SUBLANES, LRU_WIDTH), 0)
    ngroups = tc // SUBLANES
    carry0 = jnp.where(j == 0, 0.0, carry_ref[...])

    def scan_group(st, carry, reverse):
        a = a_ref[pl.ds(st, SUBLANES), :]
        u = u_ref[pl.ds(st, SUBLANES), :]
        for sh in (1, 2, 4):
            if reverse:
                ok = row < SUBLANES - sh
                shift = SUBLANES - sh
            else:
                ok = row >= sh
                shift = sh
            a_s = jnp.where(ok, pltpu.roll(a, shift, 0), 1.0)
            u_s = jnp.where(ok, pltpu.roll(u, shift, 0), 0.0)
            u = u + a * u_s
            a = a * a_s
        return u + a * carry

    @pl.when(p == 0)
    def _():
        base = chunk * tc

        def body(g, carry):
            st = pl.multiple_of(g * SUBLANES, SUBLANES)
            h = scan_group(st, carry, False)
            hf_ref[pl.ds(pl.multiple_of(base + st, SUBLANES), SUBLANES), :] = h
            return jnp.broadcast_to(h[SUBLANES - 1:SUBLANES, :], h.shape)

        carry_ref[...] = lax.fori_loop(0, ngroups, body, carry0)

    @pl.when(p == 1)
    def _():
        def body(g, carry):
            st = pl.multiple_of((ngroups - 1 - g) * SUBLANES, SUBLANES)
            h = scan_group(st, carry, True)
            h_ref[pl.ds(st, SUBLANES), :] = h
            return jnp.broadcast_to(h[0:1, :], h.shape)

        carry_ref[...] = lax.fori_loop(0, ngroups, body, carry0)
        gate = gb_ref[...]
        cdf = 0.5 * (1.0 + jnp.tanh(math.sqrt(2.0 / math.pi)
                                    * (gate + 0.044715 * (gate * gate * gate))))
        hf = hf_ref[pl.ds(pl.multiple_of(chunk * tc, SUBLANES), tc), :]
        y = (hf + h_ref[...]) * (gate * cdf)
        o_ref[...] = _rms(y, ng_ref[...]).astype(BF16)


def _lru(xb, gb, cw, cb, wg, bg, ap, ng, B, S):
    T = xb.shape[0]
    tc = LRU_CHUNK
    nc = S // tc
    hb = tc // SUBLANES
    nhalo = T // SUBLANES

    def chunk_of(p, j):
        return j + p * (nc - 1 - 2 * j)

    cur = pl.BlockSpec((tc, LRU_WIDTH), lambda b, p, j: (b * nc + chunk_of(p, j), 0))
    prev = pl.BlockSpec((SUBLANES, LRU_WIDTH),
                        lambda b, p, j: (jnp.maximum((b * nc + chunk_of(p, j)) * hb - 1, 0), 0))
    nxt = pl.BlockSpec((SUBLANES, LRU_WIDTH),
                       lambda b, p, j: (jnp.minimum((b * nc + chunk_of(p, j) + 1) * hb,
                                                    nhalo - 1), 0))
    out = pl.BlockSpec((tc, LRU_WIDTH), lambda b, p, j: (b * nc + nc - 1 - p * j, 0))
    return pl.pallas_call(
        functools.partial(_lru_kernel, nc=nc, tc=tc),
        grid=(B, 2, nc),
        in_specs=[prev, cur, nxt, cur,
                  _const_spec((4, LRU_WIDTH)), _const_spec((1, LRU_WIDTH)),
                  pl.BlockSpec((1, 2, LRU_WIDTH // 2, LRU_WIDTH), lambda b, p, j: (p, 0, 0, 0)),
                  pl.BlockSpec((1, 1, 2 * LRU_WIDTH), lambda b, p, j: (p, 0, 0)),
                  pl.BlockSpec((1, 1, LRU_WIDTH), lambda b, p, j: (p, 0, 0)),
                  _const_spec((1, LRU_WIDTH))],
        out_specs=out,
        out_shape=jax.ShapeDtypeStruct((T, LRU_WIDTH), BF16),
        scratch_shapes=[pltpu.VMEM((tc + 2 * SUBLANES, LRU_WIDTH), F32),
                        pltpu.VMEM((tc, LRU_WIDTH), F32),
                        pltpu.VMEM((tc, LRU_WIDTH), F32),
                        pltpu.VMEM((tc, LRU_WIDTH), F32),
                        pltpu.VMEM((S, LRU_WIDTH), F32),
                        pltpu.VMEM((SUBLANES, LRU_WIDTH), F32)],
        compiler_params=_params(("parallel", "arbitrary", "arbitrary"), 40),
        name="rg_lru",
    )(xb, xb, xb, gb, cw, cb, wg, bg, ap, ng)


def _mem_kv_kernel(m_ref, g_ref, wk_ref, wv_ref, k_ref, v_ref):
    m = _rms(m_ref[...], g_ref[...]).astype(BF16)
    k_ref[...] = jnp.dot(m, wk_ref[...], preferred_element_type=F32).astype(BF16)
    v_ref[...] = jnp.dot(m, wv_ref[...], preferred_element_type=F32).astype(BF16)


def _mem_kv(mem2, g, wk, wv, B):
    blk = pl.BlockSpec((N_MEM, D_MODEL), lambda b: (b, 0))
    w = _const_spec((D_MODEL, D_MODEL))
    return pl.pallas_call(
        _mem_kv_kernel,
        grid=(B,),
        in_specs=[blk, _const_spec((1, D_MODEL)), w, w],
        out_specs=[blk, blk],
        out_shape=[jax.ShapeDtypeStruct(mem2.shape, BF16)] * 2,
        compiler_params=_params(("parallel",), 24),
        name="mem_kv",
    )(mem2, g, wk, wv)


def _mix_xattn_kernel(x_ref, o_ref, r_ref, w1_ref, w2_ref, gq_ref, wq_ref, kx_ref, vx_ref,
                      wo_ref, out_ref):
    x = (x_ref[...]
         + jnp.dot(o_ref[...], w1_ref[...], preferred_element_type=F32)
         + jnp.dot(r_ref[...], w2_ref[...], preferred_element_type=F32))
    hq = _rms(x, gq_ref[...]).astype(BF16)
    qx = (jnp.dot(hq, wq_ref[...], preferred_element_type=F32)
          * (X_HEAD_DIM ** -0.5)).astype(BF16)
    nt = (((1,), (1,)), ((), ()))
    heads = []
    for h in range(X_HEADS):
        sl = slice(h * X_HEAD_DIM, (h + 1) * X_HEAD_DIM)
        s = lax.dot_general(qx[:, sl], kx_ref[:, sl], nt, preferred_element_type=F32)
        e = jnp.exp(s - jnp.max(s, axis=-1, keepdims=True))
        pr = (e * (1.0 / jnp.sum(e, axis=-1, keepdims=True))).astype(BF16)
        heads.append(jnp.dot(pr, vx_ref[:, sl], preferred_element_type=F32).astype(BF16))
    ox = jnp.concatenate(heads, axis=1)
    out_ref[...] = x + jnp.dot(ox, wo_ref[...], preferred_element_type=F32)


def _mix_xattn(x2, o, r, w_out1, w_out2, gq, wq, kx, vx, wo, S):
    T = x2.shape[0]
    tm = ROW_TILE
    per_batch = S // tm
    row = lambda n: pl.BlockSpec((tm, n), lambda i: (i, 0))
    memspec = pl.BlockSpec((N_MEM, D_MODEL), lambda i: (i // per_batch, 0))
    w = _const_spec((D_MODEL, D_MODEL))
    wh = _const_spec((DA_WIDTH, D_MODEL))
    return pl.pallas_call(
        _mix_xattn_kernel,
        grid=(T // tm,),
        in_specs=[row(D_MODEL), row(DA_WIDTH), row(LRU_WIDTH), wh, wh,
                  _const_spec((1, D_MODEL)), w, memspec, memspec, w],
        out_specs=row(D_MODEL),
        out_shape=jax.ShapeDtypeStruct((T, D_MODEL), F32),
        compiler_params=_params(("parallel",), 40),
        name="mix_xattn",
    )(x2, o, r, w_out1, w_out2, gq, wq, kx, vx, wo)


def _mlp_kernel(x_ref, g_ref, w1_ref, w2_ref, gf_ref, out_ref, *, final):
    x = x_ref[...]
    hm = _rms(x, g_ref[...]).astype(BF16)
    acc = None
    for c in range(D_FF // D_MODEL):
        sl = slice(c * D_MODEL, (c + 1) * D_MODEL)
        a = jnp.dot(hm, w1_ref[:, sl], preferred_element_type=F32)
        a = jnp.square(jnp.maximum(a, 0.0)).astype(BF16)
        d = jnp.dot(a, w2_ref[sl, :], preferred_element_type=F32)
        acc = d if acc is None else acc + d
    acc = x + acc
    if final:
        acc = _rms(acc, gf_ref[...])
    out_ref[...] = acc


def _mlp(x2, g, w1, w2, gf, final):
    T = x2.shape[0]
    tm = ROW_TILE
    row = pl.BlockSpec((tm, D_MODEL), lambda i: (i, 0))
    vec = _const_spec((1, D_MODEL))
    return pl.pallas_call(
        functools.partial(_mlp_kernel, final=final),
        grid=(T // tm,),
        in_specs=[row, vec, _const_spec(w1.shape), _const_spec(w2.shape), vec],
        out_specs=row,
        out_shape=jax.ShapeDtypeStruct((T, D_MODEL), F32),
        compiler_params=_params(("parallel",), 48),
        name="mlp",
    )(x2, g, w1, w2, gf)


def _rope_lane_tables(positions):
    half = ROPE_DIM // 2
    inv_freq = jnp.power(jnp.float32(ROPE_THETA),
                         -jnp.arange(0, ROPE_DIM, 2, dtype=F32) / ROPE_DIM)
    ang = positions.astype(F32).reshape(-1, 1) * inv_freq
    cos, sin = jnp.cos(ang), jnp.sin(ang)
    T = ang.shape[0]
    pad = DA_HEAD_DIM - ROPE_DIM
    ones = jnp.ones((T, pad), F32)
    zeros = jnp.zeros((T, pad), F32)
    zh = jnp.zeros((T, half), F32)
    rc = jnp.concatenate([cos, cos, ones], axis=1)
    rs1 = jnp.concatenate([-sin, zh, zeros], axis=1)
    rs2 = jnp.concatenate([zh, sin, zeros], axis=1)
    tile = lambda t: jnp.concatenate([t, t], axis=1)
    return tile(rc), tile(rs1), tile(rs2)


def _gate_weights(w_r, w_i):
    per_half = LRU_BLOCKS // 2

    def dense(w, lo):
        return jax.scipy.linalg.block_diag(*[w[lo + n] for n in range(per_half)])

    return jnp.stack([jnp.concatenate([dense(w_r, lo), dense(w_i, lo)], axis=1)
                      for lo in (0, per_half)]).astype(BF16)


def kernel(x, mem, positions, g_mix, w_in, da_lq1, da_lk1, da_lq2, da_lk2, da_subln_g,
           lru_conv_w, lru_conv_b, lru_w_r, lru_b_r, lru_w_i, lru_b_i, lru_a_param,
           lru_norm_g, w_out, g_xq, g_mem, xa_wq, xa_wk, xa_wv, xa_wo,
           g_mlp, mlp_w1, mlp_w2, g_final):
    B, S, D = x.shape
    depth = w_in.shape[0]
    T = B * S
    x2 = x.reshape(T, D)
    mem2 = mem.reshape(B * N_MEM, D)
    rc, rs1, rs2 = _rope_lane_tables(positions)
    vec = lambda a: a.reshape(1, -1)

    for l in range(depth):
        lam_init = 0.8 - 0.6 * float(np.exp(-0.3 * l))
        q, k, v, xb, gb = _in_proj(x2, vec(g_mix[l]), w_in[l].astype(BF16), rc, rs1, rs2)
        o = _attention(q, k, v, vec(da_lq1[l]), vec(da_lk1[l]), vec(da_lq2[l]),
                       vec(da_lk2[l]), vec(da_subln_g[l]), lam_init, B, S)
        wg = jnp.stack([_gate_weights(lru_w_r[l, d], lru_w_i[l, d]) for d in range(2)])
        bg = jnp.concatenate([lru_b_r[l].reshape(2, 1, LRU_WIDTH),
                              lru_b_i[l].reshape(2, 1, LRU_WIDTH)], axis=-1)
        r = _lru(xb, gb, lru_conv_w[l], vec(lru_conv_b[l]), wg, bg,
                 lru_a_param[l].reshape(2, 1, LRU_WIDTH), vec(lru_norm_g[l]), B, S)
        kx, vx = _mem_kv(mem2, vec(g_mem[l]), xa_wk[l].astype(BF16), xa_wv[l].astype(BF16), B)
        w_o = w_out[l].astype(BF16)
        x2 = _mix_xattn(x2, o, r, w_o[:DA_WIDTH], w_o[DA_WIDTH:], vec(g_xq[l]),
                        xa_wq[l].astype(BF16), kx, vx, xa_wo[l].astype(BF16), S)
        x2 = _mlp(x2, vec(g_mlp[l]), mlp_w1[l].astype(BF16), mlp_w2[l].astype(BF16),
                  vec(g_final), l == depth - 1)
    return x2.reshape(B, S, D)
```

```python
import functools
import math

import numpy as np
import jax
import jax.numpy as jnp
from jax import lax
from jax.experimental import pallas as pl
from jax.experimental.pallas import tpu as pltpu

F32 = jnp.float32
BF16 = jnp.bfloat16

D_MODEL = 1024
N_MEM = 256
DA_HEADS = 4
DA_HEAD_DIM = 64
DA_V_DIM = 128
DA_WIDTH = 512
LRU_WIDTH = 512
LRU_BLOCKS = 8
LRU_BLOCK = 64
LRU_C = 8.0
ROPE_THETA = 500000.0
ROPE_DIM = 16
X_HEADS = 4
X_HEAD_DIM = 256
D_FF = 4096
EPS = 1e-6

LANES = 128
SUBLANES = 8
MIB = 1024 * 1024

ROW_TILE = 512
ATTN_Q_TILE = 1024
ATTN_KEY_CHUNK = 512
ATTN_ROW_STREAMS = 8
LRU_CHUNK = 512


def _rms(x, g):
    return x * lax.rsqrt(jnp.mean(x * x, axis=-1, keepdims=True) + EPS) * g


def _const_spec(shape):
    nd = len(shape)
    return pl.BlockSpec(shape, lambda *_: (0,) * nd, pipeline_mode=pl.Buffered(1))


def _params(sem, vmem_mib):
    return pltpu.CompilerParams(dimension_semantics=sem, vmem_limit_bytes=vmem_mib * MIB)


def _in_proj_kernel(x_ref, g_ref, w_ref, rope_ref, q_ref, k_ref, v_ref, xb_ref, gb_ref):
    h = _rms(x_ref[...], g_ref[...]).astype(BF16)
    rc = rope_ref[:, 0:LANES]
    rs1 = rope_ref[:, LANES:2 * LANES]
    rs2 = rope_ref[:, 2 * LANES:3 * LANES]

    def rope(t):
        return (t * rc + pltpu.roll(t, LANES - ROPE_DIM // 2, 1) * rs1
                + pltpu.roll(t, ROPE_DIM // 2, 1) * rs2)

    zq = jnp.dot(h, w_ref[:, 0:DA_WIDTH], preferred_element_type=F32)
    zk = jnp.dot(h, w_ref[:, DA_WIDTH:2 * DA_WIDTH], preferred_element_type=F32)
    scale = DA_HEAD_DIM ** -0.5 * math.log2(math.e)
    for j in range(DA_WIDTH // LANES):
        sl = slice(j * LANES, (j + 1) * LANES)
        q_ref[:, sl] = (rope(zq[:, sl]) * scale).astype(BF16)
        k_ref[:, sl] = rope(zk[:, sl]).astype(BF16)
    v_ref[...] = jnp.dot(h, w_ref[:, 2 * DA_WIDTH:3 * DA_WIDTH],
                         preferred_element_type=F32).astype(BF16)
    xb_ref[...] = jnp.dot(h, w_ref[:, 3 * DA_WIDTH:3 * DA_WIDTH + LRU_WIDTH],
                          preferred_element_type=F32)
    gb_ref[...] = jnp.dot(h, w_ref[:, 3 * DA_WIDTH + LRU_WIDTH:],
                          preferred_element_type=F32)


def _in_proj(x2, g, w, rope):
    T = x2.shape[0]
    tm = ROW_TILE
    row = lambda n: pl.BlockSpec((tm, n), lambda i: (i, 0))
    return pl.pallas_call(
        _in_proj_kernel,
        grid=(T // tm,),
        in_specs=[row(D_MODEL), _const_spec((1, D_MODEL)), _const_spec(w.shape),
                  row(3 * LANES)],
        out_specs=[row(DA_WIDTH)] * 5,
        out_shape=[jax.ShapeDtypeStruct((T, DA_WIDTH), BF16)] * 3
        + [jax.ShapeDtypeStruct((T, LRU_WIDTH), F32)] * 2,
        compiler_params=_params(("parallel",), 40),
        name="in_proj",
    )(x2, g, w, rope)


def _attn_kernel(lq1_ref, lk1_ref, lq2_ref, lk2_ref, g_ref, q_ref, k_ref, v_ref, o_ref,
                 vext_ref, *, lam_init):
    @pl.when(pl.program_id(2) == 0)
    def _():
        vext_ref[:, 0:DA_V_DIM] = v_ref[...]
        vlane = lax.broadcasted_iota(jnp.int32, v_ref.shape, 1)
        vext_ref[:, DA_V_DIM:] = jnp.where(vlane == 0, 1.0, 0.0).astype(BF16)

    lam = (jnp.exp(jnp.sum(lq1_ref[...] * lk1_ref[...], axis=-1, keepdims=True))
           - jnp.exp(jnp.sum(lq2_ref[...] * lk2_ref[...], axis=-1, keepdims=True))
           + lam_init)
    nt = (((1,), (1,)), ((), ()))
    chunks = [slice(c, c + ATTN_KEY_CHUNK) for c in range(0, k_ref.shape[0], ATTN_KEY_CHUNK)]
    tr = q_ref.shape[0] // ATTN_ROW_STREAMS
    lane = lax.broadcasted_iota(jnp.int32, (tr, LANES), 1)
    zero = jnp.zeros((tr, LANES), BF16)

    def softmax_pv(qm):
        ss = [lax.dot_general(qm, k_ref[ks, :], nt, preferred_element_type=F32)
              for ks in chunks]
        m = ss[0]
        for s in ss[1:]:
            m = jnp.maximum(m, s)
        m = jnp.max(m, axis=-1, keepdims=True)
        acc = None
        for s, ks in zip(ss, chunks):
            e = jnp.exp2(s - m).astype(BF16)
            d = jnp.dot(e, vext_ref[ks, :], preferred_element_type=F32)
            acc = d if acc is None else acc + d
        return acc[:, 0:DA_V_DIM] * (1.0 / acc[:, DA_V_DIM:DA_V_DIM + 1])

    for r in range(ATTN_ROW_STREAMS):
        rows = slice(r * tr, (r + 1) * tr)
        q = q_ref[rows, :]
        o = (softmax_pv(jnp.where(lane < DA_HEAD_DIM, q, zero))
             - lam * softmax_pv(jnp.where(lane >= DA_HEAD_DIM, q, zero)))
        o_ref[rows, :] = (_rms(o, g_ref[...]) * (1.0 - lam_init)).astype(BF16)


def _attention(q, k, v, lq1, lk1, lq2, lk2, g, lam_init, B, S):
    T = q.shape[0]
    tq = ATTN_Q_TILE
    nq = S // tq
    vec = _const_spec((1, DA_HEAD_DIM))
    qspec = pl.BlockSpec((tq, LANES), lambda b, h, i: (b * nq + i, h))
    kvspec = pl.BlockSpec((S, LANES), lambda b, h, i: (b, h))
    return pl.pallas_call(
        functools.partial(_attn_kernel, lam_init=lam_init),
        grid=(B, DA_HEADS, nq),
        in_specs=[vec, vec, vec, vec, _const_spec((1, DA_V_DIM)), qspec, kvspec, kvspec],
        out_specs=qspec,
        out_shape=jax.ShapeDtypeStruct((T, DA_WIDTH), BF16),
        scratch_shapes=[pltpu.VMEM((S, 2 * DA_V_DIM), BF16)],
        compiler_params=_params(("arbitrary", "arbitrary", "arbitrary"), 56),
        name="diff_attn",
    )(lq1, lk1, lq2, lk2, g, q, k, v)


def _sigmoid(x):
    return 0.5 * jnp.tanh(0.5 * x) + 0.5


def _lru_kernel(xp_ref, xc_ref, xn_ref, gb_ref, cw_ref, cb_ref, wg_ref, bg_ref, ap_ref,
                ng_ref, o_ref, c_ref, a_ref, u_ref, h_ref, hf_ref, carry_ref, *, nc, tc):
    p = pl.program_id(1)
    j = pl.program_id(2)
    chunk = j + p * (nc - 1 - 2 * j)
    rows = pl.ds(pl.multiple_of(chunk * tc, SUBLANES), tc)

    @pl.when(p == 0)
    def _():
        xc = xc_ref[...]
        xe = jnp.concatenate([jnp.where(chunk == 0, 0.0, xp_ref[...]), xc,
                              jnp.where(chunk == nc - 1, 0.0, xn_ref[...])], axis=0)
        n = tc + 2 * SUBLANES

        def delayed(d):
            return pltpu.roll(xe, d % n, 0)[SUBLANES:SUBLANES + tc, :]

        cw = cw_ref[...]
        c_ref[rows, :] = (cb_ref[...] + delayed(2) * cw[0:1, :] + delayed(1) * cw[1:2, :]
                          + xc * cw[2:3, :] + delayed(-1) * cw[3:4, :])

    half = LRU_WIDTH // 2
    c = c_ref[rows, :]
    c16 = c.astype(BF16)
    g_lo = jnp.dot(c16[:, :half], wg_ref[0, 0], preferred_element_type=F32)
    g_hi = jnp.dot(c16[:, half:], wg_ref[0, 1], preferred_element_type=F32)
    bg = bg_ref[0]
    r = _sigmoid(jnp.concatenate([g_lo[:, :half], g_hi[:, :half]], axis=1) + bg[:, :LRU_WIDTH])
    i = _sigmoid(jnp.concatenate([g_lo[:, half:], g_hi[:, half:]], axis=1) + bg[:, LRU_WIDTH:])
    nap = -ap_ref[0]
    softplus = jnp.maximum(nap, 0.0) + jnp.log1p(jnp.exp(-jnp.abs(nap)))
    a = jnp.exp2(r * (softplus * (-LRU_C * math.log2(math.e))))
    a_ref[...] = a
    y = jnp.maximum(1.0 - a * a, 0.0)
    u_ref[...] = c * i * (y * lax.rsqrt(jnp.maximum(y, 1e-37)))

    row = lax.broadcasted_iota(jnp.int32, (SUBLANES, LRU_WIDTH), 0)
    ngroups = tc // SUBLANES
    carry0 = jnp.where(j == 0, 0.0, carry_ref[...])

    def scan_group(st, carry, reverse):
        a = a_ref[pl.ds(st, SUBLANES), :]
        u = u_ref[pl.ds(st, SUBLANES), :]
        for sh in (1, 2, 4):
            if reverse:
                ok = row < SUBLANES - sh
                shift = SUBLANES - sh
            else:
                ok = row >= sh
                shift = sh
            a_s = jnp.where(ok, pltpu.roll(a, shift, 0), 1.0)
            u_s = jnp.where(ok, pltpu.roll(u, shift, 0), 0.0)
            u = u + a * u_s
            a = a * a_s
        return u + a * carry

    @pl.when(p == 0)
    def _():
        base = chunk * tc

        def body(g, carry):
            st = pl.multiple_of(g * SUBLANES, SUBLANES)
            h = scan_group(st, carry, False)
            hf_ref[pl.ds(pl.multiple_of(base + st, SUBLANES), SUBLANES), :] = h
            return jnp.broadcast_to(h[SUBLANES - 1:SUBLANES, :], h.shape)

        carry_ref[...] = lax.fori_loop(0, ngroups, body, carry0, unroll=2)

    @pl.when(p == 1)
    def _():
        def body(g, carry):
            st = pl.multiple_of((ngroups - 1 - g) * SUBLANES, SUBLANES)
            h = scan_group(st, carry, True)
            h_ref[pl.ds(st, SUBLANES), :] = h
            return jnp.broadcast_to(h[0:1, :], h.shape)

        carry_ref[...] = lax.fori_loop(0, ngroups, body, carry0, unroll=2)
        gate = gb_ref[...]
        cdf = 0.5 * (1.0 + jnp.tanh(math.sqrt(2.0 / math.pi)
                                    * (gate + 0.044715 * (gate * gate * gate))))
        y = (hf_ref[rows, :] + h_ref[...]) * (gate * cdf)
        o_ref[...] = _rms(y, ng_ref[...]).astype(BF16)


def _lru(xb, gb, cw, cb, wg, bg, ap, ng, B, S):
    T = xb.shape[0]
    tc = LRU_CHUNK
    nc = S // tc
    hb = tc // SUBLANES
    nhalo = T // SUBLANES

    def x_chunk(p, j):
        return j + p * (nc - 1 - j)

    cur = pl.BlockSpec((tc, LRU_WIDTH), lambda b, p, j: (b * nc + x_chunk(p, j), 0))
    prev = pl.BlockSpec((SUBLANES, LRU_WIDTH),
                        lambda b, p, j: (jnp.maximum((b * nc + x_chunk(p, j)) * hb - 1, 0), 0))
    nxt = pl.BlockSpec((SUBLANES, LRU_WIDTH),
                       lambda b, p, j: (jnp.minimum((b * nc + x_chunk(p, j) + 1) * hb,
                                                    nhalo - 1), 0))
    out = pl.BlockSpec((tc, LRU_WIDTH), lambda b, p, j: (b * nc + nc - 1 - p * j, 0))
    return pl.pallas_call(
        functools.partial(_lru_kernel, nc=nc, tc=tc),
        grid=(B, 2, nc),
        in_specs=[prev, cur, nxt, out,
                  _const_spec((4, LRU_WIDTH)), _const_spec((1, LRU_WIDTH)),
                  pl.BlockSpec((1, 2, LRU_WIDTH // 2, LRU_WIDTH), lambda b, p, j: (p, 0, 0, 0)),
                  pl.BlockSpec((1, 1, 2 * LRU_WIDTH), lambda b, p, j: (p, 0, 0)),
                  pl.BlockSpec((1, 1, LRU_WIDTH), lambda b, p, j: (p, 0, 0)),
                  _const_spec((1, LRU_WIDTH))],
        out_specs=out,
        out_shape=jax.ShapeDtypeStruct((T, LRU_WIDTH), BF16),
        scratch_shapes=[pltpu.VMEM((S, LRU_WIDTH), F32),
                        pltpu.VMEM((tc, LRU_WIDTH), F32),
                        pltpu.VMEM((tc, LRU_WIDTH), F32),
                        pltpu.VMEM((tc, LRU_WIDTH), F32),
                        pltpu.VMEM((S, LRU_WIDTH), F32),
                        pltpu.VMEM((SUBLANES, LRU_WIDTH), F32)],
        compiler_params=_params(("parallel", "arbitrary", "arbitrary"), 40),
        name="rg_lru",
    )(xb, xb, xb, gb, cw, cb, wg, bg, ap, ng)


def _mem_kv_kernel(m_ref, g_ref, wk_ref, wv_ref, k_ref, v_ref):
    m = _rms(m_ref[...], g_ref[...]).astype(BF16)
    k_ref[...] = jnp.dot(m, wk_ref[...], preferred_element_type=F32).astype(BF16)
    v_ref[...] = jnp.dot(m, wv_ref[...], preferred_element_type=F32).astype(BF16)


def _mem_kv(mem2, g, wk, wv, B):
    blk = pl.BlockSpec((N_MEM, D_MODEL), lambda b: (b, 0))
    w = _const_spec((D_MODEL, D_MODEL))
    return pl.pallas_call(
        _mem_kv_kernel,
        grid=(B,),
        in_specs=[blk, _const_spec((1, D_MODEL)), w, w],
        out_specs=[blk, blk],
        out_shape=[jax.ShapeDtypeStruct(mem2.shape, BF16)] * 2,
        compiler_params=_params(("parallel",), 24),
        name="mem_kv",
    )(mem2, g, wk, wv)


def _mix_xattn_kernel(x_ref, o_ref, r_ref, w1_ref, w2_ref, gq_ref, wq_ref, kx_ref, vx_ref,
                      wo_ref, out_ref):
    x = (x_ref[...]
         + jnp.dot(o_ref[...], w1_ref[...], preferred_element_type=F32)
         + jnp.dot(r_ref[...], w2_ref[...], preferred_element_type=F32))
    hq = _rms(x, gq_ref[...]).astype(BF16)
    qx = (jnp.dot(hq, wq_ref[...], preferred_element_type=F32)
          * (X_HEAD_DIM ** -0.5)).astype(BF16)
    nt = (((1,), (1,)), ((), ()))
    heads = []
    for h in range(X_HEADS):
        sl = slice(h * X_HEAD_DIM, (h + 1) * X_HEAD_DIM)
        s = lax.dot_general(qx[:, sl], kx_ref[:, sl], nt, preferred_element_type=F32)
        e = jnp.exp(s - jnp.max(s, axis=-1, keepdims=True))
        pr = (e * (1.0 / jnp.sum(e, axis=-1, keepdims=True))).astype(BF16)
        heads.append(jnp.dot(pr, vx_ref[:, sl], preferred_element_type=F32).astype(BF16))
    ox = jnp.concatenate(heads, axis=1)
    out_ref[...] = x + jnp.dot(ox, wo_ref[...], preferred_element_type=F32)


def _mix_xattn(x2, o, r, w_out1, w_out2, gq, wq, kx, vx, wo, S):
    T = x2.shape[0]
    tm = ROW_TILE
    per_batch = S // tm
    row = lambda n: pl.BlockSpec((tm, n), lambda i: (i, 0))
    memspec = pl.BlockSpec((N_MEM, D_MODEL), lambda i: (i // per_batch, 0))
    w = _const_spec((D_MODEL, D_MODEL))
    wh = _const_spec((DA_WIDTH, D_MODEL))
    return pl.pallas_call(
        _mix_xattn_kernel,
        grid=(T // tm,),
        in_specs=[row(D_MODEL), row(DA_WIDTH), row(LRU_WIDTH), wh, wh,
                  _const_spec((1, D_MODEL)), w, memspec, memspec, w],
        out_specs=row(D_MODEL),
        out_shape=jax.ShapeDtypeStruct((T, D_MODEL), F32),
        compiler_params=_params(("parallel",), 40),
        name="mix_xattn",
    )(x2, o, r, w_out1, w_out2, gq, wq, kx, vx, wo)


def _mlp_kernel(x_ref, g_ref, w1_ref, w2_ref, gf_ref, out_ref, *, final):
    x = x_ref[...]
    hm = _rms(x, g_ref[...]).astype(BF16)
    acc = None
    for c in range(D_FF // D_MODEL):
        sl = slice(c * D_MODEL, (c + 1) * D_MODEL)
        a = jnp.dot(hm, w1_ref[:, sl], preferred_element_type=F32)
        a = jnp.square(jnp.maximum(a, 0.0)).astype(BF16)
        d = jnp.dot(a, w2_ref[sl, :], preferred_element_type=F32)
        acc = d if acc is None else acc + d
    acc = x + acc
    if final:
        acc = _rms(acc, gf_ref[...])
    out_ref[...] = acc


def _mlp(x2, g, w1, w2, gf, final):
    T = x2.shape[0]
    tm = ROW_TILE
    row = pl.BlockSpec((tm, D_MODEL), lambda i: (i, 0))
    vec = _const_spec((1, D_MODEL))
    return pl.pallas_call(
        functools.partial(_mlp_kernel, final=final),
        grid=(T // tm,),
        in_specs=[row, vec, _const_spec(w1.shape), _const_spec(w2.shape), vec],
        out_specs=row,
        out_shape=jax.ShapeDtypeStruct((T, D_MODEL), F32),
        compiler_params=_params(("parallel",), 48),
        name="mlp",
    )(x2, g, w1, w2, gf)


def _rope_lane_tables(positions):
    half = ROPE_DIM // 2
    inv_freq = jnp.power(jnp.float32(ROPE_THETA),
                         -jnp.arange(0, ROPE_DIM, 2, dtype=F32) / ROPE_DIM)
    ang = inv_freq.reshape(-1, 1) * positions.astype(F32).reshape(1, -1)
    src = jnp.concatenate([jnp.cos(ang), jnp.sin(ang), jnp.ones_like(ang[:1])], axis=0)
    place = np.zeros((2 * half + 1, 3 * LANES), np.float32)
    for lane in range(LANES):
        m = lane % DA_HEAD_DIM
        if m < half:
            place[m, lane] = 1.0
            place[half + m, LANES + lane] = -1.0
        elif m < ROPE_DIM:
            place[m - half, lane] = 1.0
            place[m, 2 * LANES + lane] = 1.0
        else:
            place[2 * half, lane] = 1.0
    return lax.dot_general(src, jnp.asarray(place), (((0,), (0,)), ((), ())),
                           precision=lax.Precision.HIGHEST)


def _gate_weights(w_r, w_i):
    per_half = LRU_BLOCKS // 2

    def dense(w, lo):
        return jax.scipy.linalg.block_diag(*[w[lo + n] for n in range(per_half)])

    return jnp.stack([jnp.concatenate([dense(w_r, lo), dense(w_i, lo)], axis=1)
                      for lo in (0, per_half)]).astype(BF16)


def kernel(x, mem, positions, g_mix, w_in, da_lq1, da_lk1, da_lq2, da_lk2, da_subln_g,
           lru_conv_w, lru_conv_b, lru_w_r, lru_b_r, lru_w_i, lru_b_i, lru_a_param,
           lru_norm_g, w_out, g_xq, g_mem, xa_wq, xa_wk, xa_wv, xa_wo,
           g_mlp, mlp_w1, mlp_w2, g_final):
    B, S, D = x.shape
    depth = w_in.shape[0]
    T = B * S
    x2 = x.reshape(T, D)
    mem2 = mem.reshape(B * N_MEM, D)
    rope = _rope_lane_tables(positions)
    vec = lambda a: a.reshape(1, -1)

    for l in range(depth):
        lam_init = 0.8 - 0.6 * float(np.exp(-0.3 * l))
        q, k, v, xb, gb = _in_proj(x2, vec(g_mix[l]), w_in[l].astype(BF16), rope)
        o = _attention(q, k, v, vec(da_lq1[l]), vec(da_lk1[l]), vec(da_lq2[l]),
                       vec(da_lk2[l]), vec(da_subln_g[l]), lam_init, B, S)
        wg = jnp.stack([_gate_weights(lru_w_r[l, d], lru_w_i[l, d]) for d in range(2)])
        bg = jnp.concatenate([lru_b_r[l].reshape(2, 1, LRU_WIDTH),
                              lru_b_i[l].reshape(2, 1, LRU_WIDTH)], axis=-1)
        r = _lru(xb, gb, lru_conv_w[l], vec(lru_conv_b[l]), wg, bg,
                 lru_a_param[l].reshape(2, 1, LRU_WIDTH), vec(lru_norm_g[l]), B, S)
        kx, vx = _mem_kv(mem2, vec(g_mem[l]), xa_wk[l].astype(BF16), xa_wv[l].astype(BF16), B)
        w_o = w_out[l].astype(BF16)
        x2 = _mix_xattn(x2, o, r, w_o[:DA_WIDTH], w_o[DA_WIDTH:], vec(g_xq[l]),
                        xa_wq[l].astype(BF16), kx, vx, xa_wo[l].astype(BF16), S)
        x2 = _mlp(x2, vec(g_mlp[l]), mlp_w1[l].astype(BF16), mlp_w2[l].astype(BF16),
                  vec(g_final), l == depth - 1)
    return x2.reshape(B, S, D)
```

```python
import functools
import math

import numpy as np
import jax
import jax.numpy as jnp
from jax import lax
from jax.experimental import pallas as pl
from jax.experimental.pallas import tpu as pltpu

F32 = jnp.float32
BF16 = jnp.bfloat16

D_MODEL = 1024
N_MEM = 256
DA_HEADS = 4
DA_HEAD_DIM = 64
DA_V_DIM = 128
DA_WIDTH = 512
LRU_WIDTH = 512
LRU_BLOCKS = 8
LRU_BLOCK = 64
LRU_C = 8.0
ROPE_THETA = 500000.0
ROPE_DIM = 16
X_HEADS = 4
X_HEAD_DIM = 256
D_FF = 4096
EPS = 1e-6

LANES = 128
SUBLANES = 8
MIB = 1024 * 1024

ROW_TILE = 512
ATTN_Q_TILE = 1024
ATTN_KEY_CHUNK = 512
ATTN_ROW_STREAMS = 8
LRU_CHUNK = 512
MIX_ROW_TILE = 1024
LRU_UNROLL = 8


def _rms(x, g):
    return x * lax.rsqrt(jnp.mean(x * x, axis=-1, keepdims=True) + EPS) * g


def _const_spec(shape):
    nd = len(shape)
    return pl.BlockSpec(shape, lambda *_: (0,) * nd, pipeline_mode=pl.Buffered(1))


def _params(sem, vmem_mib):
    return pltpu.CompilerParams(dimension_semantics=sem, vmem_limit_bytes=vmem_mib * MIB)


def _in_proj_kernel(x_ref, g_ref, w_ref, rope_ref, q_ref, k_ref, v_ref, xb_ref, gb_ref):
    h = _rms(x_ref[...], g_ref[...]).astype(BF16)
    rc = rope_ref[:, 0:LANES]
    rs1 = rope_ref[:, LANES:2 * LANES]
    rs2 = rope_ref[:, 2 * LANES:3 * LANES]

    def rope(t):
        return (t * rc + pltpu.roll(t, LANES - ROPE_DIM // 2, 1) * rs1
                + pltpu.roll(t, ROPE_DIM // 2, 1) * rs2)

    zq = jnp.dot(h, w_ref[:, 0:DA_WIDTH], preferred_element_type=F32)
    zk = jnp.dot(h, w_ref[:, DA_WIDTH:2 * DA_WIDTH], preferred_element_type=F32)
    scale = DA_HEAD_DIM ** -0.5 * math.log2(math.e)
    for j in range(DA_WIDTH // LANES):
        sl = slice(j * LANES, (j + 1) * LANES)
        q_ref[:, sl] = (rope(zq[:, sl]) * scale).astype(BF16)
        k_ref[:, sl] = rope(zk[:, sl]).astype(BF16)
    v_ref[...] = jnp.dot(h, w_ref[:, 2 * DA_WIDTH:3 * DA_WIDTH],
                         preferred_element_type=F32).astype(BF16)
    xb_ref[...] = jnp.dot(h, w_ref[:, 3 * DA_WIDTH:3 * DA_WIDTH + LRU_WIDTH],
                          preferred_element_type=F32)
    gb_ref[...] = jnp.dot(h, w_ref[:, 3 * DA_WIDTH + LRU_WIDTH:],
                          preferred_element_type=F32)


def _in_proj(x2, g, w, rope):
    T = x2.shape[0]
    tm = ROW_TILE
    row = lambda n: pl.BlockSpec((tm, n), lambda i: (i, 0))
    return pl.pallas_call(
        _in_proj_kernel,
        grid=(T // tm,),
        in_specs=[row(D_MODEL), _const_spec((1, D_MODEL)), _const_spec(w.shape),
                  row(3 * LANES)],
        out_specs=[row(DA_WIDTH)] * 5,
        out_shape=[jax.ShapeDtypeStruct((T, DA_WIDTH), BF16)] * 3
        + [jax.ShapeDtypeStruct((T, LRU_WIDTH), F32)] * 2,
        compiler_params=_params(("parallel",), 40),
        name="in_proj",
    )(x2, g, w, rope)


def _attn_kernel(lq1_ref, lk1_ref, lq2_ref, lk2_ref, g_ref, q_ref, k_ref, v_ref, o_ref,
                 vext_ref, *, lam_init):
    @pl.when(pl.program_id(2) == 0)
    def _():
        vext_ref[:, 0:DA_V_DIM] = v_ref[...]
        vlane = lax.broadcasted_iota(jnp.int32, v_ref.shape, 1)
        vext_ref[:, DA_V_DIM:] = jnp.where(vlane == 0, 1.0, 0.0).astype(BF16)

    lam = (jnp.exp(jnp.sum(lq1_ref[...] * lk1_ref[...], axis=-1, keepdims=True))
           - jnp.exp(jnp.sum(lq2_ref[...] * lk2_ref[...], axis=-1, keepdims=True))
           + lam_init)
    nt = (((1,), (1,)), ((), ()))
    chunks = [slice(c, c + ATTN_KEY_CHUNK) for c in range(0, k_ref.shape[0], ATTN_KEY_CHUNK)]
    tr = q_ref.shape[0] // ATTN_ROW_STREAMS
    lane = lax.broadcasted_iota(jnp.int32, (tr, LANES), 1)
    zero = jnp.zeros((tr, LANES), BF16)

    def softmax_pv(qm):
        ss = [lax.dot_general(qm, k_ref[ks, :], nt, preferred_element_type=F32)
              for ks in chunks]
        m = ss[0]
        for s in ss[1:]:
            m = jnp.maximum(m, s)
        m = jnp.max(m, axis=-1, keepdims=True)
        acc = None
        for s, ks in zip(ss, chunks):
            e = jnp.exp2(s - m).astype(BF16)
            d = jnp.dot(e, vext_ref[ks, :], preferred_element_type=F32)
            acc = d if acc is None else acc + d
        return acc[:, 0:DA_V_DIM] * (1.0 / acc[:, DA_V_DIM:DA_V_DIM + 1])

    for r in range(ATTN_ROW_STREAMS):
        rows = slice(r * tr, (r + 1) * tr)
        q = q_ref[rows, :]
        o = (softmax_pv(jnp.where(lane < DA_HEAD_DIM, q, zero))
             - lam * softmax_pv(jnp.where(lane >= DA_HEAD_DIM, q, zero)))
        o_ref[rows, :] = (_rms(o, g_ref[...]) * (1.0 - lam_init)).astype(BF16)


def _attention(q, k, v, lq1, lk1, lq2, lk2, g, lam_init, B, S):
    T = q.shape[0]
    tq = ATTN_Q_TILE
    nq = S // tq
    vec = _const_spec((1, DA_HEAD_DIM))
    qspec = pl.BlockSpec((tq, LANES), lambda b, h, i: (b * nq + i, h))
    kvspec = pl.BlockSpec((S, LANES), lambda b, h, i: (b, h))
    return pl.pallas_call(
        functools.partial(_attn_kernel, lam_init=lam_init),
        grid=(B, DA_HEADS, nq),
        in_specs=[vec, vec, vec, vec, _const_spec((1, DA_V_DIM)), qspec, kvspec, kvspec],
        out_specs=qspec,
        out_shape=jax.ShapeDtypeStruct((T, DA_WIDTH), BF16),
        scratch_shapes=[pltpu.VMEM((S, 2 * DA_V_DIM), BF16)],
        compiler_params=_params(("arbitrary", "arbitrary", "arbitrary"), 56),
        name="diff_attn",
    )(lq1, lk1, lq2, lk2, g, q, k, v)


def _lru_kernel(xp_ref, xc_ref, xn_ref, gb_ref, cw_ref, cb_ref, wg_ref, bg_ref, ap_ref,
                ng_ref, o_ref, c_ref, a_ref, u_ref, h_ref, hf_ref, carry_ref, *, nc, tc):
    p = pl.program_id(1)
    j = pl.program_id(2)
    chunk = j + p * (nc - 1 - 2 * j)
    rows = pl.ds(pl.multiple_of(chunk * tc, SUBLANES), tc)

    @pl.when(p == 0)
    def _():
        xc = xc_ref[...]
        xe = jnp.concatenate([jnp.where(chunk == 0, 0.0, xp_ref[...]), xc,
                              jnp.where(chunk == nc - 1, 0.0, xn_ref[...])], axis=0)
        n = tc + 2 * SUBLANES

        def delayed(d):
            return pltpu.roll(xe, d % n, 0)[SUBLANES:SUBLANES + tc, :]

        cw = cw_ref[...]
        c_ref[rows, :] = (cb_ref[...] + delayed(2) * cw[0:1, :] + delayed(1) * cw[1:2, :]
                          + xc * cw[2:3, :] + delayed(-1) * cw[3:4, :])

    half = LRU_WIDTH // 2
    c = c_ref[rows, :]
    c16 = c.astype(BF16)
    g_lo = jnp.dot(c16[:, :half], wg_ref[0, 0], preferred_element_type=F32)
    g_hi = jnp.dot(c16[:, half:], wg_ref[0, 1], preferred_element_type=F32)
    bg = bg_ref[0]
    tr = jnp.tanh(jnp.concatenate([g_lo[:, :half], g_hi[:, :half]], axis=1) + bg[:, :LRU_WIDTH])
    ti = jnp.tanh(jnp.concatenate([g_lo[:, half:], g_hi[:, half:]], axis=1) + bg[:, LRU_WIDTH:])
    nap = -ap_ref[0]
    softplus = jnp.maximum(nap, 0.0) + jnp.log1p(jnp.exp(-jnp.abs(nap)))
    kh = softplus * (-0.5 * LRU_C * math.log2(math.e))
    a = jnp.exp2(tr * kh + kh)
    a_ref[...] = a
    y = jnp.maximum(1.0 - a * a, 1e-37)
    u_ref[...] = (c * (y * lax.rsqrt(y))) * (0.5 * ti + 0.5)

    row = lax.broadcasted_iota(jnp.int32, (SUBLANES, LRU_WIDTH), 0)
    ngroups = tc // SUBLANES
    carry0 = jnp.where(j == 0, 0.0, carry_ref[...])

    def scan_group(st, carry, reverse):
        a = a_ref[pl.ds(st, SUBLANES), :]
        u = u_ref[pl.ds(st, SUBLANES), :]
        for sh in (1, 2, 4):
            if reverse:
                ok = row < SUBLANES - sh
                shift = SUBLANES - sh
            else:
                ok = row >= sh
                shift = sh
            a_s = jnp.where(ok, pltpu.roll(a, shift, 0), 1.0)
            u_s = jnp.where(ok, pltpu.roll(u, shift, 0), 0.0)
            u = u + a * u_s
            a = a * a_s
        return u + a * carry

    @pl.when(p == 0)
    def _():
        base = chunk * tc

        def body(g, carry):
            st = pl.multiple_of(g * SUBLANES, SUBLANES)
            h = scan_group(st, carry, False)
            hf_ref[pl.ds(pl.multiple_of(base + st, SUBLANES), SUBLANES), :] = h
            return jnp.broadcast_to(h[SUBLANES - 1:SUBLANES, :], h.shape)

        carry_ref[...] = lax.fori_loop(0, ngroups, body, carry0, unroll=LRU_UNROLL)

    @pl.when(p == 1)
    def _():
        def body(g, carry):
            st = pl.multiple_of((ngroups - 1 - g) * SUBLANES, SUBLANES)
            h = scan_group(st, carry, True)
            h_ref[pl.ds(st, SUBLANES), :] = h
            return jnp.broadcast_to(h[0:1, :], h.shape)

        carry_ref[...] = lax.fori_loop(0, ngroups, body, carry0, unroll=LRU_UNROLL)
        gate = gb_ref[...]
        cdf = 0.5 * (1.0 + jnp.tanh(math.sqrt(2.0 / math.pi)
                                    * (gate + 0.044715 * (gate * gate * gate))))
        y = (hf_ref[rows, :] + h_ref[...]) * (gate * cdf)
        o_ref[...] = _rms(y, ng_ref[...]).astype(BF16)


def _lru(xb, gb, cw, cb, wg, bg, ap, ng, B, S):
    T = xb.shape[0]
    tc = LRU_CHUNK
    nc = S // tc
    hb = tc // SUBLANES
    nhalo = T // SUBLANES

    def x_chunk(p, j):
        return j + p * (nc - 1 - j)

    cur = pl.BlockSpec((tc, LRU_WIDTH), lambda b, p, j: (b * nc + x_chunk(p, j), 0))
    prev = pl.BlockSpec((SUBLANES, LRU_WIDTH),
                        lambda b, p, j: (jnp.maximum((b * nc + x_chunk(p, j)) * hb - 1, 0), 0))
    nxt = pl.BlockSpec((SUBLANES, LRU_WIDTH),
                       lambda b, p, j: (jnp.minimum((b * nc + x_chunk(p, j) + 1) * hb,
                                                    nhalo - 1), 0))
    out = pl.BlockSpec((tc, LRU_WIDTH), lambda b, p, j: (b * nc + nc - 1 - p * j, 0))
    return pl.pallas_call(
        functools.partial(_lru_kernel, nc=nc, tc=tc),
        grid=(B, 2, nc),
        in_specs=[prev, cur, nxt, out,
                  _const_spec((4, LRU_WIDTH)), _const_spec((1, LRU_WIDTH)),
                  pl.BlockSpec((1, 2, LRU_WIDTH // 2, LRU_WIDTH), lambda b, p, j: (p, 0, 0, 0)),
                  pl.BlockSpec((1, 1, 2 * LRU_WIDTH), lambda b, p, j: (p, 0, 0)),
                  pl.BlockSpec((1, 1, LRU_WIDTH), lambda b, p, j: (p, 0, 0)),
                  _const_spec((1, LRU_WIDTH))],
        out_specs=out,
        out_shape=jax.ShapeDtypeStruct((T, LRU_WIDTH), BF16),
        scratch_shapes=[pltpu.VMEM((S, LRU_WIDTH), F32),
                        pltpu.VMEM((tc, LRU_WIDTH), F32),
                        pltpu.VMEM((tc, LRU_WIDTH), F32),
                        pltpu.VMEM((tc, LRU_WIDTH), F32),
                        pltpu.VMEM((S, LRU_WIDTH), F32),
                        pltpu.VMEM((SUBLANES, LRU_WIDTH), F32)],
        compiler_params=_params(("parallel", "arbitrary", "arbitrary"), 40),
        name="rg_lru",
    )(xb, xb, xb, gb, cw, cb, wg, bg, ap, ng)


def _mem_kv_kernel(m_ref, g_ref, wk_ref, wv_ref, k_ref, v_ref):
    m = _rms(m_ref[...], g_ref[...]).astype(BF16)
    k_ref[...] = jnp.dot(m, wk_ref[...], preferred_element_type=F32).astype(BF16)
    v_ref[...] = jnp.dot(m, wv_ref[...], preferred_element_type=F32).astype(BF16)


def _mem_kv(mem2, g, wk, wv, B):
    blk = pl.BlockSpec((N_MEM, D_MODEL), lambda b: (b, 0))
    w = _const_spec((D_MODEL, D_MODEL))
    return pl.pallas_call(
        _mem_kv_kernel,
        grid=(B,),
        in_specs=[blk, _const_spec((1, D_MODEL)), w, w],
        out_specs=[blk, blk],
        out_shape=[jax.ShapeDtypeStruct(mem2.shape, BF16)] * 2,
        compiler_params=_params(("parallel",), 24),
        name="mem_kv",
    )(mem2, g, wk, wv)


def _mix_xattn_kernel(x_ref, o_ref, r_ref, w1_ref, w2_ref, gq_ref, wq_ref, kx_ref, vx_ref,
                      wo_ref, out_ref):
    nt = (((1,), (1,)), ((), ()))
    x = (x_ref[...]
         + jnp.dot(o_ref[...], w1_ref[...], preferred_element_type=F32)
         + jnp.dot(r_ref[...], w2_ref[...], preferred_element_type=F32))
    hq = _rms(x, gq_ref[...]).astype(BF16)
    qx = (jnp.dot(hq, wq_ref[...], preferred_element_type=F32)
          * (X_HEAD_DIM ** -0.5 * math.log2(math.e))).astype(BF16)
    heads = []
    for h in range(X_HEADS):
        sl = slice(h * X_HEAD_DIM, (h + 1) * X_HEAD_DIM)
        s = lax.dot_general(qx[:, sl], kx_ref[:, sl], nt, preferred_element_type=F32)
        e = jnp.exp2(s - jnp.max(s, axis=-1, keepdims=True))
        pr = (e * (1.0 / jnp.sum(e, axis=-1, keepdims=True))).astype(BF16)
        heads.append(jnp.dot(pr, vx_ref[:, sl], preferred_element_type=F32).astype(BF16))
    ox = jnp.concatenate(heads, axis=1)
    out_ref[...] = x + jnp.dot(ox, wo_ref[...], preferred_element_type=F32)


def _mix_xattn(x2, o, r, w_out1, w_out2, gq, wq, kx, vx, wo, S):
    T = x2.shape[0]
    tm = MIX_ROW_TILE
    per_batch = S // tm
    row = lambda n: pl.BlockSpec((tm, n), lambda i: (i, 0))
    memspec = pl.BlockSpec((N_MEM, D_MODEL), lambda i: (i // per_batch, 0))
    w = _const_spec((D_MODEL, D_MODEL))
    wh = _const_spec((DA_WIDTH, D_MODEL))
    return pl.pallas_call(
        _mix_xattn_kernel,
        grid=(T // tm,),
        in_specs=[row(D_MODEL), row(DA_WIDTH), row(LRU_WIDTH), wh, wh,
                  _const_spec((1, D_MODEL)), w, memspec, memspec, w],
        out_specs=row(D_MODEL),
        out_shape=jax.ShapeDtypeStruct((T, D_MODEL), F32),
        compiler_params=_params(("parallel",), 40),
        name="mix_xattn",
    )(x2, o, r, w_out1, w_out2, gq, wq, kx, vx, wo)


def _mlp_kernel(x_ref, g_ref, w1_ref, w2_ref, gf_ref, out_ref, *, final):
    x = x_ref[...]
    hm = _rms(x, g_ref[...]).astype(BF16)
    acc = None
    for c in range(D_FF // D_MODEL):
        sl = slice(c * D_MODEL, (c + 1) * D_MODEL)
        a = jnp.dot(hm, w1_ref[:, sl], preferred_element_type=F32)
        a = jnp.square(jnp.maximum(a, 0.0)).astype(BF16)
        d = jnp.dot(a, w2_ref[sl, :], preferred_element_type=F32)
        acc = d if acc is None else acc + d
    acc = x + acc
    if final:
        acc = _rms(acc, gf_ref[...])
    out_ref[...] = acc


def _mlp(x2, g, w1, w2, gf, final):
    T = x2.shape[0]
    tm = ROW_TILE
    row = pl.BlockSpec((tm, D_MODEL), lambda i: (i, 0))
    vec = _const_spec((1, D_MODEL))
    return pl.pallas_call(
        functools.partial(_mlp_kernel, final=final),
        grid=(T // tm,),
        in_specs=[row, vec, _const_spec(w1.shape), _const_spec(w2.shape), vec],
        out_specs=row,
        out_shape=jax.ShapeDtypeStruct((T, D_MODEL), F32),
        compiler_params=_params(("parallel",), 48),
        name="mlp",
    )(x2, g, w1, w2, gf)


def _rope_lane_tables(positions):
    half = ROPE_DIM // 2
    inv_freq = jnp.power(jnp.float32(ROPE_THETA),
                         -jnp.arange(0, ROPE_DIM, 2, dtype=F32) / ROPE_DIM)
    ang = inv_freq.reshape(-1, 1) * positions.astype(F32).reshape(1, -1)
    src = jnp.concatenate([jnp.cos(ang), jnp.sin(ang), jnp.ones_like(ang[:1])], axis=0)
    place = np.zeros((2 * half + 1, 3 * LANES), np.float32)
    for lane in range(LANES):
        m = lane % DA_HEAD_DIM
        if m < half:
            place[m, lane] = 1.0
            place[half + m, LANES + lane] = -1.0
        elif m < ROPE_DIM:
            place[m - half, lane] = 1.0
            place[m, 2 * LANES + lane] = 1.0
        else:
            place[2 * half, lane] = 1.0
    return lax.dot_general(src, jnp.asarray(place), (((0,), (0,)), ((), ())),
                           precision=lax.Precision.HIGHEST)


def _gate_weights(w_r, w_i):
    per_half = LRU_BLOCKS // 2

    def dense(w, lo):
        return jax.scipy.linalg.block_diag(*[w[lo + n] for n in range(per_half)])

    return (0.5 * jnp.stack([jnp.concatenate([dense(w_r, lo), dense(w_i, lo)], axis=1)
                             for lo in (0, per_half)])).astype(BF16)


def kernel(x, mem, positions, g_mix, w_in, da_lq1, da_lk1, da_lq2, da_lk2, da_subln_g,
           lru_conv_w, lru_conv_b, lru_w_r, lru_b_r, lru_w_i, lru_b_i, lru_a_param,
           lru_norm_g, w_out, g_xq, g_mem, xa_wq, xa_wk, xa_wv, xa_wo,
           g_mlp, mlp_w1, mlp_w2, g_final):
    B, S, D = x.shape
    depth = w_in.shape[0]
    T = B * S
    x2 = x.reshape(T, D)
    mem2 = mem.reshape(B * N_MEM, D)
    rope = _rope_lane_tables(positions)
    vec = lambda a: a.reshape(1, -1)

    for l in range(depth):
        lam_init = 0.8 - 0.6 * float(np.exp(-0.3 * l))
        q, k, v, xb, gb = _in_proj(x2, vec(g_mix[l]), w_in[l].astype(BF16), rope)
        o = _attention(q, k, v, vec(da_lq1[l]), vec(da_lk1[l]), vec(da_lq2[l]),
                       vec(da_lk2[l]), vec(da_subln_g[l]), lam_init, B, S)
        wg = jnp.stack([_gate_weights(lru_w_r[l, d], lru_w_i[l, d]) for d in range(2)])
        bg = 0.5 * jnp.concatenate([lru_b_r[l].reshape(2, 1, LRU_WIDTH),
                                    lru_b_i[l].reshape(2, 1, LRU_WIDTH)], axis=-1)
        r = _lru(xb, gb, lru_conv_w[l], vec(lru_conv_b[l]), wg, bg,
                 lru_a_param[l].reshape(2, 1, LRU_WIDTH), vec(lru_norm_g[l]), B, S)
        kx, vx = _mem_kv(mem2, vec(g_mem[l]), xa_wk[l].astype(BF16), xa_wv[l].astype(BF16), B)
        w_o = w_out[l].astype(BF16)
        x2 = _mix_xattn(x2, o, r, w_o[:DA_WIDTH], w_o[DA_WIDTH:], vec(g_xq[l]),
                        xa_wq[l].astype(BF16), kx, vx, xa_wo[l].astype(BF16), S)
        x2 = _mlp(x2, vec(g_mlp[l]), mlp_w1[l].astype(BF16), mlp_w2[l].astype(BF16),
                  vec(g_final), l == depth - 1)
    return x2.reshape(B, S, D)
```

```python
import functools
import math

import numpy as np
import jax
import jax.numpy as jnp
from jax import lax
from jax.experimental import pallas as pl
from jax.experimental.pallas import tpu as pltpu

F32 = jnp.float32
BF16 = jnp.bfloat16

D_MODEL = 1024
N_MEM = 256
DA_HEADS = 4
DA_HEAD_DIM = 64
DA_V_DIM = 128
DA_WIDTH = 512
LRU_WIDTH = 512
LRU_BLOCKS = 8
LRU_BLOCK = 64
LRU_C = 8.0
ROPE_THETA = 500000.0
ROPE_DIM = 16
X_HEADS = 4
X_HEAD_DIM = 256
D_FF = 4096
EPS = 1e-6

LANES = 128
SUBLANES = 8
MIB = 1024 * 1024

ROW_TILE = 512
ATTN_Q_TILE = 1024
ATTN_KEY_CHUNK = 512
ATTN_ROW_STREAMS = 8
LRU_CHUNK = 512
MIX_ROW_TILE = 1024
LRU_UNROLL = 8
MLP_FF_CHUNK = 1024


def _rms(x, g):
    return x * lax.rsqrt(jnp.mean(x * x, axis=-1, keepdims=True) + EPS) * g


def _const_spec(shape):
    nd = len(shape)
    return pl.BlockSpec(shape, lambda *_: (0,) * nd, pipeline_mode=pl.Buffered(1))


def _params(sem, vmem_mib):
    return pltpu.CompilerParams(dimension_semantics=sem, vmem_limit_bytes=vmem_mib * MIB)


def _in_proj_kernel(x_ref, g_ref, w_ref, rope_ref, q_ref, k_ref, v_ref, xb_ref, gb_ref):
    h = _rms(x_ref[...], g_ref[...]).astype(BF16)
    rc = rope_ref[:, 0:LANES]
    rs1 = rope_ref[:, LANES:2 * LANES]
    rs2 = rope_ref[:, 2 * LANES:3 * LANES]

    def rope(t):
        return (t * rc + pltpu.roll(t, LANES - ROPE_DIM // 2, 1) * rs1
                + pltpu.roll(t, ROPE_DIM // 2, 1) * rs2)

    zq = jnp.dot(h, w_ref[:, 0:DA_WIDTH], preferred_element_type=F32)
    zk = jnp.dot(h, w_ref[:, DA_WIDTH:2 * DA_WIDTH], preferred_element_type=F32)
    scale = DA_HEAD_DIM ** -0.5 * math.log2(math.e)
    for j in range(DA_WIDTH // LANES):
        sl = slice(j * LANES, (j + 1) * LANES)
        q_ref[:, sl] = (rope(zq[:, sl]) * scale).astype(BF16)
        k_ref[:, sl] = rope(zk[:, sl]).astype(BF16)
    v_ref[...] = jnp.dot(h, w_ref[:, 2 * DA_WIDTH:3 * DA_WIDTH],
                         preferred_element_type=F32).astype(BF16)
    xb_ref[...] = jnp.dot(h, w_ref[:, 3 * DA_WIDTH:3 * DA_WIDTH + LRU_WIDTH],
                          preferred_element_type=F32)
    gb_ref[...] = jnp.dot(h, w_ref[:, 3 * DA_WIDTH + LRU_WIDTH:],
                          preferred_element_type=F32)


def _in_proj(x2, g, w, rope):
    T = x2.shape[0]
    tm = ROW_TILE
    row = lambda n: pl.BlockSpec((tm, n), lambda i: (i, 0))
    return pl.pallas_call(
        _in_proj_kernel,
        grid=(T // tm,),
        in_specs=[row(D_MODEL), _const_spec((1, D_MODEL)), _const_spec(w.shape),
                  row(3 * LANES)],
        out_specs=[row(DA_WIDTH)] * 5,
        out_shape=[jax.ShapeDtypeStruct((T, DA_WIDTH), BF16)] * 3
        + [jax.ShapeDtypeStruct((T, LRU_WIDTH), F32)] * 2,
        compiler_params=_params(("parallel",), 40),
        name="in_proj",
    )(x2, g, w, rope)


def _attn_kernel(lq1_ref, lk1_ref, lq2_ref, lk2_ref, g_ref, q_ref, k_ref, v_ref, o_ref,
                 vext_ref, *, lam_init):
    @pl.when(pl.program_id(2) == 0)
    def _():
        vext_ref[:, 0:DA_V_DIM] = v_ref[...]
        vext_ref[:, DA_V_DIM:] = jnp.ones(v_ref.shape, BF16)

    lam = (jnp.exp(jnp.sum(lq1_ref[...] * lk1_ref[...], axis=-1, keepdims=True))
           - jnp.exp(jnp.sum(lq2_ref[...] * lk2_ref[...], axis=-1, keepdims=True))
           + lam_init)
    nt = (((1,), (1,)), ((), ()))
    chunks = [slice(c, c + ATTN_KEY_CHUNK) for c in range(0, k_ref.shape[0], ATTN_KEY_CHUNK)]
    tr = q_ref.shape[0] // ATTN_ROW_STREAMS
    lane = lax.broadcasted_iota(jnp.int32, (tr, LANES), 1)
    zero = jnp.zeros((tr, LANES), BF16)

    def softmax_pv(qm):
        ss = [lax.dot_general(qm, k_ref[ks, :], nt, preferred_element_type=F32)
              for ks in chunks]
        m = ss[0]
        for s in ss[1:]:
            m = jnp.maximum(m, s)
        m = jnp.max(m, axis=-1, keepdims=True)
        acc = None
        for s, ks in zip(ss, chunks):
            e = jnp.exp2(s - m).astype(BF16)
            d = jnp.dot(e, vext_ref[ks, :], preferred_element_type=F32)
            acc = d if acc is None else acc + d
        return acc[:, 0:DA_V_DIM] * (1.0 / acc[:, DA_V_DIM:])

    for r in range(ATTN_ROW_STREAMS):
        rows = slice(r * tr, (r + 1) * tr)
        q = q_ref[rows, :]
        o = (softmax_pv(jnp.where(lane < DA_HEAD_DIM, q, zero))
             - lam * softmax_pv(jnp.where(lane >= DA_HEAD_DIM, q, zero)))
        o_ref[rows, :] = (_rms(o, g_ref[...]) * (1.0 - lam_init)).astype(BF16)


def _attention(q, k, v, lq1, lk1, lq2, lk2, g, lam_init, B, S):
    T = q.shape[0]
    tq = ATTN_Q_TILE
    nq = S // tq
    vec = _const_spec((1, DA_HEAD_DIM))
    qspec = pl.BlockSpec((tq, LANES), lambda b, h, i: (b * nq + i, h))
    kvspec = pl.BlockSpec((S, LANES), lambda b, h, i: (b, h))
    return pl.pallas_call(
        functools.partial(_attn_kernel, lam_init=lam_init),
        grid=(B, DA_HEADS, nq),
        in_specs=[vec, vec, vec, vec, _const_spec((1, DA_V_DIM)), qspec, kvspec, kvspec],
        out_specs=qspec,
        out_shape=jax.ShapeDtypeStruct((T, DA_WIDTH), BF16),
        scratch_shapes=[pltpu.VMEM((S, 2 * DA_V_DIM), BF16)],
        compiler_params=_params(("arbitrary", "arbitrary", "arbitrary"), 56),
        name="diff_attn",
    )(lq1, lk1, lq2, lk2, g, q, k, v)


def _lru_kernel(xp_ref, xc_ref, xn_ref, gb_ref, cw_ref, cb_ref, wg_ref, bg_ref, ap_ref,
                ng_ref, o_ref, c_ref, a_ref, u_ref, h_ref, hf_ref, carry_ref, *, nc, tc):
    p = pl.program_id(1)
    j = pl.program_id(2)
    chunk = j + p * (nc - 1 - 2 * j)
    rows = pl.ds(pl.multiple_of(chunk * tc, SUBLANES), tc)

    @pl.when(p == 0)
    def _():
        xc = xc_ref[...]
        xe = jnp.concatenate([jnp.where(chunk == 0, 0.0, xp_ref[...]), xc,
                              jnp.where(chunk == nc - 1, 0.0, xn_ref[...])], axis=0)
        n = tc + 2 * SUBLANES

        def delayed(d):
            return pltpu.roll(xe, d % n, 0)[SUBLANES:SUBLANES + tc, :]

        cw = cw_ref[...]
        c_ref[rows, :] = (cb_ref[...] + delayed(2) * cw[0:1, :] + delayed(1) * cw[1:2, :]
                          + xc * cw[2:3, :] + delayed(-1) * cw[3:4, :])

    half = LRU_WIDTH // 2
    c = c_ref[rows, :]
    c16 = c.astype(BF16)
    g_lo = jnp.dot(c16[:, :half], wg_ref[0, 0], preferred_element_type=F32)
    g_hi = jnp.dot(c16[:, half:], wg_ref[0, 1], preferred_element_type=F32)
    bg = bg_ref[0]
    tr = jnp.tanh(jnp.concatenate([g_lo[:, :half], g_hi[:, :half]], axis=1) + bg[:, :LRU_WIDTH])
    ti = jnp.tanh(jnp.concatenate([g_lo[:, half:], g_hi[:, half:]], axis=1) + bg[:, LRU_WIDTH:])
    nap = -ap_ref[0]
    softplus = jnp.maximum(nap, 0.0) + jnp.log1p(jnp.exp(-jnp.abs(nap)))
    kh = softplus * (-0.5 * LRU_C * math.log2(math.e))
    a = jnp.exp2(tr * kh + kh)
    a_ref[...] = a
    y = jnp.maximum(1.0 - a * a, 1e-37)
    u_ref[...] = (c * (y * lax.rsqrt(y))) * (0.5 * ti + 0.5)

    row = lax.broadcasted_iota(jnp.int32, (SUBLANES, LRU_WIDTH), 0)
    ngroups = tc // SUBLANES
    carry0 = jnp.where(j == 0, 0.0, carry_ref[...])

    def scan_group(st, carry, reverse):
        a = a_ref[pl.ds(st, SUBLANES), :]
        u = u_ref[pl.ds(st, SUBLANES), :]
        for sh in (1, 2, 4):
            if reverse:
                ok = row < SUBLANES - sh
                shift = SUBLANES - sh
            else:
                ok = row >= sh
                shift = sh
            a_s = jnp.where(ok, pltpu.roll(a, shift, 0), 1.0)
            u_s = jnp.where(ok, pltpu.roll(u, shift, 0), 0.0)
            u = u + a * u_s
            a = a * a_s
        return u + a * carry

    @pl.when(p == 0)
    def _():
        base = chunk * tc

        def body(g, carry):
            st = pl.multiple_of(g * SUBLANES, SUBLANES)
            h = scan_group(st, carry, False)
            hf_ref[pl.ds(pl.multiple_of(base + st, SUBLANES), SUBLANES), :] = h
            return jnp.broadcast_to(h[SUBLANES - 1:SUBLANES, :], h.shape)

        carry_ref[...] = lax.fori_loop(0, ngroups, body, carry0, unroll=LRU_UNROLL)

    @pl.when(p == 1)
    def _():
        def body(g, carry):
            st = pl.multiple_of((ngroups - 1 - g) * SUBLANES, SUBLANES)
            h = scan_group(st, carry, True)
            h_ref[pl.ds(st, SUBLANES), :] = h
            return jnp.broadcast_to(h[0:1, :], h.shape)

        carry_ref[...] = lax.fori_loop(0, ngroups, body, carry0, unroll=LRU_UNROLL)
        gate = gb_ref[...]
        cdf = 0.5 * (1.0 + jnp.tanh(math.sqrt(2.0 / math.pi)
                                    * (gate + 0.044715 * (gate * gate * gate))))
        y = (hf_ref[rows, :] + h_ref[...]) * (gate * cdf)
        o_ref[...] = _rms(y, ng_ref[...]).astype(BF16)


def _lru(xb, gb, cw, cb, wg, bg, ap, ng, B, S):
    T = xb.shape[0]
    tc = LRU_CHUNK
    nc = S // tc
    hb = tc // SUBLANES
    nhalo = T // SUBLANES

    def x_chunk(p, j):
        return j + p * (nc - 1 - j)

    cur = pl.BlockSpec((tc, LRU_WIDTH), lambda b, p, j: (b * nc + x_chunk(p, j), 0))
    prev = pl.BlockSpec((SUBLANES, LRU_WIDTH),
                        lambda b, p, j: (jnp.maximum((b * nc + x_chunk(p, j)) * hb - 1, 0), 0))
    nxt = pl.BlockSpec((SUBLANES, LRU_WIDTH),
                       lambda b, p, j: (jnp.minimum((b * nc + x_chunk(p, j) + 1) * hb,
                                                    nhalo - 1), 0))
    out = pl.BlockSpec((tc, LRU_WIDTH), lambda b, p, j: (b * nc + nc - 1 - p * j, 0))
    return pl.pallas_call(
        functools.partial(_lru_kernel, nc=nc, tc=tc),
        grid=(B, 2, nc),
        in_specs=[prev, cur, nxt, out,
                  _const_spec((4, LRU_WIDTH)), _const_spec((1, LRU_WIDTH)),
                  pl.BlockSpec((1, 2, LRU_WIDTH // 2, LRU_WIDTH), lambda b, p, j: (p, 0, 0, 0)),
                  pl.BlockSpec((1, 1, 2 * LRU_WIDTH), lambda b, p, j: (p, 0, 0)),
                  pl.BlockSpec((1, 1, LRU_WIDTH), lambda b, p, j: (p, 0, 0)),
                  _const_spec((1, LRU_WIDTH))],
        out_specs=out,
        out_shape=jax.ShapeDtypeStruct((T, LRU_WIDTH), BF16),
        scratch_shapes=[pltpu.VMEM((S, LRU_WIDTH), F32),
                        pltpu.VMEM((tc, LRU_WIDTH), F32),
                        pltpu.VMEM((tc, LRU_WIDTH), F32),
                        pltpu.VMEM((tc, LRU_WIDTH), F32),
                        pltpu.VMEM((S, LRU_WIDTH), F32),
                        pltpu.VMEM((SUBLANES, LRU_WIDTH), F32)],
        compiler_params=_params(("parallel", "arbitrary", "arbitrary"), 40),
        name="rg_lru",
    )(xb, xb, xb, gb, cw, cb, wg, bg, ap, ng)


def _mem_kv_kernel(m_ref, g_ref, wk_ref, wv_ref, k_ref, v_ref):
    m = _rms(m_ref[...], g_ref[...]).astype(BF16)
    k_ref[...] = jnp.dot(m, wk_ref[...], preferred_element_type=F32).astype(BF16)
    v_ref[...] = jnp.dot(m, wv_ref[...], preferred_element_type=F32).astype(BF16)


def _mem_kv(mem2, g, wk, wv, B):
    blk = pl.BlockSpec((N_MEM, D_MODEL), lambda b: (b, 0))
    w = _const_spec((D_MODEL, D_MODEL))
    return pl.pallas_call(
        _mem_kv_kernel,
        grid=(B,),
        in_specs=[blk, _const_spec((1, D_MODEL)), w, w],
        out_specs=[blk, blk],
        out_shape=[jax.ShapeDtypeStruct(mem2.shape, BF16)] * 2,
        compiler_params=_params(("parallel",), 24),
        name="mem_kv",
    )(mem2, g, wk, wv)


def _mix_xattn_kernel(x_ref, o_ref, r_ref, w1_ref, w2_ref, gq_ref, wq_ref, kx_ref, vx_ref,
                      wo_ref, out_ref):
    nt = (((1,), (1,)), ((), ()))
    x = (x_ref[...]
         + jnp.dot(o_ref[...], w1_ref[...], preferred_element_type=F32)
         + jnp.dot(r_ref[...], w2_ref[...], preferred_element_type=F32))
    hq = _rms(x, gq_ref[...]).astype(BF16)
    qx = (jnp.dot(hq, wq_ref[...], preferred_element_type=F32)
          * (X_HEAD_DIM ** -0.5 * math.log2(math.e))).astype(BF16)
    heads = []
    for h in range(X_HEADS):
        sl = slice(h * X_HEAD_DIM, (h + 1) * X_HEAD_DIM)
        s = lax.dot_general(qx[:, sl], kx_ref[:, sl], nt, preferred_element_type=F32)
        e = jnp.exp2(s - jnp.max(s, axis=-1, keepdims=True))
        pr = (e * (1.0 / jnp.sum(e, axis=-1, keepdims=True))).astype(BF16)
        heads.append(jnp.dot(pr, vx_ref[:, sl], preferred_element_type=F32).astype(BF16))
    ox = jnp.concatenate(heads, axis=1)
    out_ref[...] = x + jnp.dot(ox, wo_ref[...], preferred_element_type=F32)


def _mix_xattn(x2, o, r, w_out1, w_out2, gq, wq, kx, vx, wo, S):
    T = x2.shape[0]
    tm = MIX_ROW_TILE
    per_batch = S // tm
    row = lambda n: pl.BlockSpec((tm, n), lambda i: (i, 0))
    memspec = pl.BlockSpec((N_MEM, D_MODEL), lambda i: (i // per_batch, 0))
    w = _const_spec((D_MODEL, D_MODEL))
    wh = _const_spec((DA_WIDTH, D_MODEL))
    return pl.pallas_call(
        _mix_xattn_kernel,
        grid=(T // tm,),
        in_specs=[row(D_MODEL), row(DA_WIDTH), row(LRU_WIDTH), wh, wh,
                  _const_spec((1, D_MODEL)), w, memspec, memspec, w],
        out_specs=row(D_MODEL),
        out_shape=jax.ShapeDtypeStruct((T, D_MODEL), F32),
        compiler_params=_params(("parallel",), 40),
        name="mix_xattn",
    )(x2, o, r, w_out1, w_out2, gq, wq, kx, vx, wo)


def _mlp_kernel(x_ref, g_ref, w1_ref, w2_ref, gf_ref, out_ref, *, final):
    x = x_ref[...]
    hm = _rms(x, g_ref[...]).astype(BF16)
    acc = None
    for c in range(D_FF // MLP_FF_CHUNK):
        sl = slice(c * MLP_FF_CHUNK, (c + 1) * MLP_FF_CHUNK)
        a = jnp.dot(hm, w1_ref[:, sl], preferred_element_type=F32)
        a = jnp.square(jnp.maximum(a, 0.0)).astype(BF16)
        d = jnp.dot(a, w2_ref[sl, :], preferred_element_type=F32)
        acc = d if acc is None else acc + d
    acc = x + acc
    if final:
        acc = _rms(acc, gf_ref[...])
    out_ref[...] = acc


def _mlp(x2, g, w1, w2, gf, final):
    T = x2.shape[0]
    tm = ROW_TILE
    row = pl.BlockSpec((tm, D_MODEL), lambda i: (i, 0))
    vec = _const_spec((1, D_MODEL))
    return pl.pallas_call(
        functools.partial(_mlp_kernel, final=final),
        grid=(T // tm,),
        in_specs=[row, vec, _const_spec(w1.shape), _const_spec(w2.shape), vec],
        out_specs=row,
        out_shape=jax.ShapeDtypeStruct((T, D_MODEL), F32),
        compiler_params=_params(("parallel",), 48),
        name="mlp",
    )(x2, g, w1, w2, gf)


def _rope_lane_tables(positions):
    half = ROPE_DIM // 2
    inv_freq = jnp.power(jnp.float32(ROPE_THETA),
                         -jnp.arange(0, ROPE_DIM, 2, dtype=F32) / ROPE_DIM)
    ang = inv_freq.reshape(-1, 1) * positions.astype(F32).reshape(1, -1)
    src = jnp.concatenate([jnp.cos(ang), jnp.sin(ang), jnp.ones_like(ang[:1])], axis=0)
    place = np.zeros((2 * half + 1, 3 * LANES), np.float32)
    for lane in range(LANES):
        m = lane % DA_HEAD_DIM
        if m < half:
            place[m, lane] = 1.0
            place[half + m, LANES + lane] = -1.0
        elif m < ROPE_DIM:
            place[m - half, lane] = 1.0
            place[m, 2 * LANES + lane] = 1.0
        else:
            place[2 * half, lane] = 1.0
    return lax.dot_general(src, jnp.asarray(place), (((0,), (0,)), ((), ())),
                           precision=lax.Precision.HIGHEST)


def _gate_weights(w_r, w_i):
    per_half = LRU_BLOCKS // 2

    def dense(w, lo):
        return jax.scipy.linalg.block_diag(*[w[lo + n] for n in range(per_half)])

    return (0.5 * jnp.stack([jnp.concatenate([dense(w_r, lo), dense(w_i, lo)], axis=1)
                             for lo in (0, per_half)])).astype(BF16)


def kernel(x, mem, positions, g_mix, w_in, da_lq1, da_lk1, da_lq2, da_lk2, da_subln_g,
           lru_conv_w, lru_conv_b, lru_w_r, lru_b_r, lru_w_i, lru_b_i, lru_a_param,
           lru_norm_g, w_out, g_xq, g_mem, xa_wq, xa_wk, xa_wv, xa_wo,
           g_mlp, mlp_w1, mlp_w2, g_final):
    B, S, D = x.shape
    depth = w_in.shape[0]
    T = B * S
    x2 = x.reshape(T, D)
    mem2 = mem.reshape(B * N_MEM, D)
    rope = _rope_lane_tables(positions)
    vec = lambda a: a.reshape(1, -1)

    for l in range(depth):
        lam_init = 0.8 - 0.6 * float(np.exp(-0.3 * l))
        q, k, v, xb, gb = _in_proj(x2, vec(g_mix[l]), w_in[l].astype(BF16), rope)
        o = _attention(q, k, v, vec(da_lq1[l]), vec(da_lk1[l]), vec(da_lq2[l]),
                       vec(da_lk2[l]), vec(da_subln_g[l]), lam_init, B, S)
        wg = jnp.stack([_gate_weights(lru_w_r[l, d], lru_w_i[l, d]) for d in range(2)])
        bg = 0.5 * jnp.concatenate([lru_b_r[l].reshape(2, 1, LRU_WIDTH),
                                    lru_b_i[l].reshape(2, 1, LRU_WIDTH)], axis=-1)
        r = _lru(xb, gb, lru_conv_w[l], vec(lru_conv_b[l]), wg, bg,
                 lru_a_param[l].reshape(2, 1, LRU_WIDTH), vec(lru_norm_g[l]), B, S)
        kx, vx = _mem_kv(mem2, vec(g_mem[l]), xa_wk[l].astype(BF16), xa_wv[l].astype(BF16), B)
        w_o = w_out[l].astype(BF16)
        x2 = _mix_xattn(x2, o, r, w_o[:DA_WIDTH], w_o[DA_WIDTH:], vec(g_xq[l]),
                        xa_wq[l].astype(BF16), kx, vx, xa_wo[l].astype(BF16), S)
        x2 = _mlp(x2, vec(g_mlp[l]), mlp_w1[l].astype(BF16), mlp_w2[l].astype(BF16),
                  vec(g_final), l == depth - 1)
    return x2.reshape(B, S, D)
```

```python
import functools
import math

import numpy as np
import jax
import jax.numpy as jnp
from jax import lax
from jax.experimental import pallas as pl
from jax.experimental.pallas import tpu as pltpu

F32 = jnp.float32
BF16 = jnp.bfloat16

D_MODEL = 1024
N_MEM = 256
DA_HEADS = 4
DA_HEAD_DIM = 64
DA_V_DIM = 128
DA_WIDTH = 512
LRU_WIDTH = 512
LRU_BLOCKS = 8
LRU_BLOCK = 64
LRU_C = 8.0
ROPE_THETA = 500000.0
ROPE_DIM = 16
X_HEADS = 4
X_HEAD_DIM = 256
D_FF = 4096
EPS = 1e-6

LANES = 128
SUBLANES = 8
MIB = 1024 * 1024

ROW_TILE = 512
ATTN_Q_TILE = 1024
ATTN_KEY_CHUNK = 512
ATTN_ROW_STREAMS = 8
LRU_CHUNK = 512
MIX_ROW_TILE = 1024
LRU_UNROLL = 8
LRU_HALO = 16
LRU_DEINTERLEAVE_PIECE = 256
MLP_FF_CHUNK = 1024


def _rms(x, g):
    return x * lax.rsqrt(jnp.mean(x * x, axis=-1, keepdims=True) + EPS) * g


def _const_spec(shape):
    nd = len(shape)
    return pl.BlockSpec(shape, lambda *_: (0,) * nd, pipeline_mode=pl.Buffered(1))


def _params(sem, vmem_mib):
    return pltpu.CompilerParams(dimension_semantics=sem, vmem_limit_bytes=vmem_mib * MIB)


def _in_proj_kernel(x_ref, g_ref, w_ref, rope_ref, q_ref, k_ref, v_ref, xb_ref, gb_ref):
    h = _rms(x_ref[...], g_ref[...]).astype(BF16)
    rc = rope_ref[:, 0:LANES]
    rs1 = rope_ref[:, LANES:2 * LANES]
    rs2 = rope_ref[:, 2 * LANES:3 * LANES]

    def rope(t):
        return (t * rc + pltpu.roll(t, LANES - ROPE_DIM // 2, 1) * rs1
                + pltpu.roll(t, ROPE_DIM // 2, 1) * rs2)

    zq = jnp.dot(h, w_ref[:, 0:DA_WIDTH], preferred_element_type=F32)
    zk = jnp.dot(h, w_ref[:, DA_WIDTH:2 * DA_WIDTH], preferred_element_type=F32)
    scale = DA_HEAD_DIM ** -0.5 * math.log2(math.e)
    for j in range(DA_WIDTH // LANES):
        sl = slice(j * LANES, (j + 1) * LANES)
        q_ref[:, sl] = (rope(zq[:, sl]) * scale).astype(BF16)
        k_ref[:, sl] = rope(zk[:, sl]).astype(BF16)
    v_ref[...] = jnp.dot(h, w_ref[:, 2 * DA_WIDTH:3 * DA_WIDTH],
                         preferred_element_type=F32).astype(BF16)
    xb_ref[...] = jnp.dot(h, w_ref[:, 3 * DA_WIDTH:3 * DA_WIDTH + LRU_WIDTH],
                          preferred_element_type=F32).astype(BF16)
    gb_ref[...] = jnp.dot(h, w_ref[:, 3 * DA_WIDTH + LRU_WIDTH:],
                          preferred_element_type=F32)


def _in_proj(x2, g, w, rope):
    T = x2.shape[0]
    tm = ROW_TILE
    row = lambda n: pl.BlockSpec((tm, n), lambda i: (i, 0))
    return pl.pallas_call(
        _in_proj_kernel,
        grid=(T // tm,),
        in_specs=[row(D_MODEL), _const_spec((1, D_MODEL)), _const_spec(w.shape),
                  row(3 * LANES)],
        out_specs=[row(DA_WIDTH)] * 5,
        out_shape=[jax.ShapeDtypeStruct((T, DA_WIDTH), BF16)] * 3
        + [jax.ShapeDtypeStruct((T, LRU_WIDTH), BF16), jax.ShapeDtypeStruct((T, LRU_WIDTH), F32)],
        compiler_params=_params(("parallel",), 40),
        name="in_proj",
    )(x2, g, w, rope)


def _attn_kernel(lq1_ref, lk1_ref, lq2_ref, lk2_ref, g_ref, q_ref, k_ref, v_ref, o_ref,
                 vext_ref, *, lam_init):
    @pl.when(pl.program_id(2) == 0)
    def _():
        vext_ref[:, 0:DA_V_DIM] = v_ref[...]
        vext_ref[:, DA_V_DIM:] = jnp.ones(v_ref.shape, BF16)

    lam = (jnp.exp(jnp.sum(lq1_ref[...] * lk1_ref[...], axis=-1, keepdims=True))
           - jnp.exp(jnp.sum(lq2_ref[...] * lk2_ref[...], axis=-1, keepdims=True))
           + lam_init)
    nt = (((1,), (1,)), ((), ()))
    chunks = [slice(c, c + ATTN_KEY_CHUNK) for c in range(0, k_ref.shape[0], ATTN_KEY_CHUNK)]
    tr = q_ref.shape[0] // ATTN_ROW_STREAMS
    lane = lax.broadcasted_iota(jnp.int32, (tr, LANES), 1)
    zero = jnp.zeros((tr, LANES), BF16)

    def softmax_pv(qm):
        ss = [lax.dot_general(qm, k_ref[ks, :], nt, preferred_element_type=F32)
              for ks in chunks]
        m = ss[0]
        for s in ss[1:]:
            m = jnp.maximum(m, s)
        m = jnp.max(m, axis=-1, keepdims=True)
        acc = None
        for s, ks in zip(ss, chunks):
            e = jnp.exp2(s - m).astype(BF16)
            d = jnp.dot(e, vext_ref[ks, :], preferred_element_type=F32)
            acc = d if acc is None else acc + d
        return acc[:, 0:DA_V_DIM] * (1.0 / acc[:, DA_V_DIM:])

    for r in range(ATTN_ROW_STREAMS):
        rows = slice(r * tr, (r + 1) * tr)
        q = q_ref[rows, :]
        o = (softmax_pv(jnp.where(lane < DA_HEAD_DIM, q, zero))
             - lam * softmax_pv(jnp.where(lane >= DA_HEAD_DIM, q, zero)))
        o_ref[rows, :] = (_rms(o, g_ref[...]) * (1.0 - lam_init)).astype(BF16)


def _attention(q, k, v, lq1, lk1, lq2, lk2, g, lam_init, B, S):
    T = q.shape[0]
    tq = ATTN_Q_TILE
    nq = S // tq
    vec = _const_spec((1, DA_HEAD_DIM))
    qspec = pl.BlockSpec((tq, LANES), lambda b, h, i: (b * nq + i, h))
    kvspec = pl.BlockSpec((S, LANES), lambda b, h, i: (b, h))
    return pl.pallas_call(
        functools.partial(_attn_kernel, lam_init=lam_init),
        grid=(B, DA_HEADS, nq),
        in_specs=[vec, vec, vec, vec, _const_spec((1, DA_V_DIM)), qspec, kvspec, kvspec],
        out_specs=qspec,
        out_shape=jax.ShapeDtypeStruct((T, DA_WIDTH), BF16),
        scratch_shapes=[pltpu.VMEM((S, 2 * DA_V_DIM), BF16)],
        compiler_params=_params(("arbitrary", "arbitrary", "arbitrary"), 56),
        name="diff_attn",
    )(lq1, lk1, lq2, lk2, g, q, k, v)


def _interleave_matrix(tc):
    seg_len = tc // SUBLANES
    m = np.zeros((tc, tc), np.float32)
    for t in range(tc):
        m[(t % seg_len) * SUBLANES + t // seg_len, t] = 1.0
    return m


def _lru_kernel(xp_ref, xc_ref, xn_ref, gb_ref, cw_ref, cb_ref, wg_ref, bg_ref, ap_ref,
                ng_ref, il_ref, dil_ref, o_ref, c_ref, a_ref, u_ref, h_ref, pc_ref, hf_ref,
                carry_ref, *, nc, tc):
    p = pl.program_id(1)
    j = pl.program_id(2)
    chunk = j + p * (nc - 1 - 2 * j)
    rows = pl.ds(pl.multiple_of(chunk * tc, SUBLANES), tc)
    ngroups = tc // SUBLANES
    row = lax.broadcasted_iota(jnp.int32, (SUBLANES, LRU_WIDTH), 0)

    @pl.when(p == 0)
    def _():
        x = jnp.dot(il_ref[...], xc_ref[...], preferred_element_type=F32)
        prev = jnp.where(chunk == 0, 0.0, xp_ref[...].astype(F32))
        nxt = jnp.where(chunk == nc - 1, 0.0, xn_ref[...].astype(F32))

        def from_prev_segment(g, first):
            return jnp.where(row == 0, jnp.broadcast_to(first, g.shape), pltpu.roll(g, 1, 0))

        def from_next_segment(g, last):
            return jnp.where(row == SUBLANES - 1, jnp.broadcast_to(last, g.shape),
                             pltpu.roll(g, SUBLANES - 1, 0))

        xe = jnp.concatenate(
            [from_prev_segment(x[tc - 2 * SUBLANES:tc - SUBLANES], prev[LRU_HALO - 2:LRU_HALO - 1]),
             from_prev_segment(x[tc - SUBLANES:tc], prev[LRU_HALO - 1:LRU_HALO]),
             x,
             from_next_segment(x[0:SUBLANES], nxt[0:1])], axis=0)
        cw = cw_ref[...]
        c_ref[rows, :] = (cb_ref[...] + xe[0:tc] * cw[0:1, :]
                          + xe[SUBLANES:tc + SUBLANES] * cw[1:2, :]
                          + xe[2 * SUBLANES:tc + 2 * SUBLANES] * cw[2:3, :]
                          + xe[3 * SUBLANES:tc + 3 * SUBLANES] * cw[3:4, :])

    half = LRU_WIDTH // 2
    c = c_ref[rows, :]
    c16 = c.astype(BF16)
    g_lo = jnp.dot(c16[:, :half], wg_ref[0, 0], preferred_element_type=F32)
    g_hi = jnp.dot(c16[:, half:], wg_ref[0, 1], preferred_element_type=F32)
    bg = bg_ref[0]
    tr = jnp.tanh(jnp.concatenate([g_lo[:, :half], g_hi[:, :half]], axis=1) + bg[:, :LRU_WIDTH])
    ti = jnp.tanh(jnp.concatenate([g_lo[:, half:], g_hi[:, half:]], axis=1) + bg[:, LRU_WIDTH:])
    nap = -ap_ref[0]
    softplus = jnp.maximum(nap, 0.0) + jnp.log1p(jnp.exp(-jnp.abs(nap)))
    kh = softplus * (-0.5 * LRU_C * math.log2(math.e))
    a = jnp.exp2(tr * kh + kh)
    a_ref[...] = a
    y = jnp.maximum(1.0 - a * a, 1e-37)
    u_ref[...] = (c * (y * lax.rsqrt(y))) * (0.5 * ti + 0.5)

    end_prev = jnp.where(j == 0, 0.0, carry_ref[...])

    def scan(reverse):
        def body(g, carry):
            hcur, pcur = carry
            st = pl.multiple_of(((ngroups - 1 - g) if reverse else g) * SUBLANES, SUBLANES)
            av = a_ref[pl.ds(st, SUBLANES), :]
            hcur = av * hcur + u_ref[pl.ds(st, SUBLANES), :]
            pcur = av * pcur
            h_ref[pl.ds(st, SUBLANES), :] = hcur
            pc_ref[pl.ds(st, SUBLANES), :] = pcur
            return hcur, pcur

        start = (jnp.zeros((SUBLANES, LRU_WIDTH), F32), jnp.ones((SUBLANES, LRU_WIDTH), F32))
        hend, pend = lax.fori_loop(0, ngroups, body, start, unroll=LRU_UNROLL)
        shift = SUBLANES - 1 if reverse else 1
        first = (row == SUBLANES - 1) if reverse else (row == 0)
        cin = pltpu.roll(end_prev, shift, 0)
        for _ in range(SUBLANES - 1):
            cin = jnp.where(first, cin, pltpu.roll(hend + pend * cin, shift, 0))
        carry_ref[...] = hend + pend * cin
        return cin

    def states(cin, sl):
        n = (sl.stop - sl.start) // SUBLANES
        local = h_ref[sl, :].reshape(n, SUBLANES, LRU_WIDTH)
        weight = pc_ref[sl, :].reshape(n, SUBLANES, LRU_WIDTH)
        return (local + weight * cin[None]).reshape(n * SUBLANES, LRU_WIDTH)

    @pl.when(p == 0)
    def _():
        hf_ref[rows, :] = states(scan(False), slice(0, tc))

    @pl.when(p == 1)
    def _():
        cin = scan(True)
        hsum = None
        for k0 in range(0, tc, LRU_DEINTERLEAVE_PIECE):
            sl = slice(k0, k0 + LRU_DEINTERLEAVE_PIECE)
            hf = hf_ref[pl.ds(pl.multiple_of(chunk * tc + k0, SUBLANES), LRU_DEINTERLEAVE_PIECE), :]
            d = jnp.dot(dil_ref[:, sl], (states(cin, sl) + hf).astype(BF16),
                        preferred_element_type=F32)
            hsum = d if hsum is None else hsum + d
        gate = gb_ref[...]
        cdf = 0.5 * (1.0 + jnp.tanh(math.sqrt(2.0 / math.pi)
                                    * (gate + 0.044715 * (gate * gate * gate))))
        o_ref[...] = _rms(hsum * (gate * cdf), ng_ref[...]).astype(BF16)


def _lru(xb, gb, cw, cb, wg, bg, ap, ng, B, S):
    T = xb.shape[0]
    tc = LRU_CHUNK
    nc = S // tc
    hb = tc // LRU_HALO
    nhalo = T // LRU_HALO
    il = jnp.asarray(_interleave_matrix(tc), BF16)

    def x_chunk(p, j):
        return j + p * (nc - 1 - j)

    cur = pl.BlockSpec((tc, LRU_WIDTH), lambda b, p, j: (b * nc + x_chunk(p, j), 0))
    prev = pl.BlockSpec((LRU_HALO, LRU_WIDTH),
                        lambda b, p, j: (jnp.maximum((b * nc + x_chunk(p, j)) * hb - 1, 0), 0))
    nxt = pl.BlockSpec((LRU_HALO, LRU_WIDTH),
                       lambda b, p, j: (jnp.minimum((b * nc + x_chunk(p, j) + 1) * hb,
                                                    nhalo - 1), 0))
    out = pl.BlockSpec((tc, LRU_WIDTH), lambda b, p, j: (b * nc + nc - 1 - p * j, 0))
    chunk_scratch = pltpu.VMEM((tc, LRU_WIDTH), F32)
    seq_scratch = pltpu.VMEM((S, LRU_WIDTH), F32)
    return pl.pallas_call(
        functools.partial(_lru_kernel, nc=nc, tc=tc),
        grid=(B, 2, nc),
        in_specs=[prev, cur, nxt, out,
                  _const_spec((4, LRU_WIDTH)), _const_spec((1, LRU_WIDTH)),
                  pl.BlockSpec((1, 2, LRU_WIDTH // 2, LRU_WIDTH), lambda b, p, j: (p, 0, 0, 0)),
                  pl.BlockSpec((1, 1, 2 * LRU_WIDTH), lambda b, p, j: (p, 0, 0)),
                  pl.BlockSpec((1, 1, LRU_WIDTH), lambda b, p, j: (p, 0, 0)),
                  _const_spec((1, LRU_WIDTH)), _const_spec((tc, tc)), _const_spec((tc, tc))],
        out_specs=out,
        out_shape=jax.ShapeDtypeStruct((T, LRU_WIDTH), BF16),
        scratch_shapes=[seq_scratch, chunk_scratch, chunk_scratch, chunk_scratch, chunk_scratch,
                        seq_scratch, pltpu.VMEM((SUBLANES, LRU_WIDTH), F32)],
        compiler_params=_params(("parallel", "arbitrary", "arbitrary"), 48),
        name="rg_lru",
    )(xb, xb, xb, gb, cw, cb, wg, bg, ap, ng, il, il.T)


def _mem_kv_kernel(m_ref, g_ref, wk_ref, wv_ref, k_ref, v_ref):
    m = _rms(m_ref[...], g_ref[...]).astype(BF16)
    k_ref[...] = jnp.dot(m, wk_ref[...], preferred_element_type=F32).astype(BF16)
    v_ref[...] = jnp.dot(m, wv_ref[...], preferred_element_type=F32).astype(BF16)


def _mem_kv(mem2, g, wk, wv, B):
    blk = pl.BlockSpec((N_MEM, D_MODEL), lambda b: (b, 0))
    w = _const_spec((D_MODEL, D_MODEL))
    return pl.pallas_call(
        _mem_kv_kernel,
        grid=(B,),
        in_specs=[blk, _const_spec((1, D_MODEL)), w, w],
        out_specs=[blk, blk],
        out_shape=[jax.ShapeDtypeStruct(mem2.shape, BF16)] * 2,
        compiler_params=_params(("parallel",), 24),
        name="mem_kv",
    )(mem2, g, wk, wv)


def _mix_xattn_kernel(x_ref, o_ref, r_ref, w1_ref, w2_ref, gq_ref, wq_ref, kx_ref, vx_ref,
                      wo_ref, out_ref):
    nt = (((1,), (1,)), ((), ()))
    x = (x_ref[...]
         + jnp.dot(o_ref[...], w1_ref[...], preferred_element_type=F32)
         + jnp.dot(r_ref[...], w2_ref[...], preferred_element_type=F32))
    hq = _rms(x, gq_ref[...]).astype(BF16)
    qx = (jnp.dot(hq, wq_ref[...], preferred_element_type=F32)
          * (X_HEAD_DIM ** -0.5 * math.log2(math.e))).astype(BF16)
    heads = []
    for h in range(X_HEADS):
        sl = slice(h * X_HEAD_DIM, (h + 1) * X_HEAD_DIM)
        s = lax.dot_general(qx[:, sl], kx_ref[:, sl], nt, preferred_element_type=F32)
        e = jnp.exp2(s - jnp.max(s, axis=-1, keepdims=True))
        pr = (e * (1.0 / jnp.sum(e, axis=-1, keepdims=True))).astype(BF16)
        heads.append(jnp.dot(pr, vx_ref[:, sl], preferred_element_type=F32).astype(BF16))
    ox = jnp.concatenate(heads, axis=1)
    out_ref[...] = x + jnp.dot(ox, wo_ref[...], preferred_element_type=F32)


def _mix_xattn(x2, o, r, w_out1, w_out2, gq, wq, kx, vx, wo, S):
    T = x2.shape[0]
    tm = MIX_ROW_TILE
    per_batch = S // tm
    row = lambda n: pl.BlockSpec((tm, n), lambda i: (i, 0))
    memspec = pl.BlockSpec((N_MEM, D_MODEL), lambda i: (i // per_batch, 0))
    w = _const_spec((D_MODEL, D_MODEL))
    wh = _const_spec((DA_WIDTH, D_MODEL))
    return pl.pallas_call(
        _mix_xattn_kernel,
        grid=(T // tm,),
        in_specs=[row(D_MODEL), row(DA_WIDTH), row(LRU_WIDTH), wh, wh,
                  _const_spec((1, D_MODEL)), w, memspec, memspec, w],
        out_specs=row(D_MODEL),
        out_shape=jax.ShapeDtypeStruct((T, D_MODEL), F32),
        compiler_params=_params(("parallel",), 40),
        name="mix_xattn",
    )(x2, o, r, w_out1, w_out2, gq, wq, kx, vx, wo)


def _mlp_kernel(x_ref, g_ref, w1_ref, w2_ref, gf_ref, out_ref, *, final):
    x = x_ref[...]
    hm = _rms(x, g_ref[...]).astype(BF16)
    acc = None
    for c in range(D_FF // MLP_FF_CHUNK):
        sl = slice(c * MLP_FF_CHUNK, (c + 1) * MLP_FF_CHUNK)
        a = jnp.dot(hm, w1_ref[:, sl], preferred_element_type=F32)
        a = jnp.square(jnp.maximum(a, 0.0)).astype(BF16)
        d = jnp.dot(a, w2_ref[sl, :], preferred_element_type=F32)
        acc = d if acc is None else acc + d
    acc = x + acc
    if final:
        acc = _rms(acc, gf_ref[...])
    out_ref[...] = acc


def _mlp(x2, g, w1, w2, gf, final):
    T = x2.shape[0]
    tm = ROW_TILE
    row = pl.BlockSpec((tm, D_MODEL), lambda i: (i, 0))
    vec = _const_spec((1, D_MODEL))
    return pl.pallas_call(
        functools.partial(_mlp_kernel, final=final),
        grid=(T // tm,),
        in_specs=[row, vec, _const_spec(w1.shape), _const_spec(w2.shape), vec],
        out_specs=row,
        out_shape=jax.ShapeDtypeStruct((T, D_MODEL), F32),
        compiler_params=_params(("parallel",), 48),
        name="mlp",
    )(x2, g, w1, w2, gf)


def _rope_lane_tables(positions):
    half = ROPE_DIM // 2
    inv_freq = jnp.power(jnp.float32(ROPE_THETA),
                         -jnp.arange(0, ROPE_DIM, 2, dtype=F32) / ROPE_DIM)
    ang = inv_freq.reshape(-1, 1) * positions.astype(F32).reshape(1, -1)
    src = jnp.concatenate([jnp.cos(ang), jnp.sin(ang), jnp.ones_like(ang[:1])], axis=0)
    place = np.zeros((2 * half + 1, 3 * LANES), np.float32)
    for lane in range(LANES):
        m = lane % DA_HEAD_DIM
        if m < half:
            place[m, lane] = 1.0
            place[half + m, LANES + lane] = -1.0
        elif m < ROPE_DIM:
            place[m - half, lane] = 1.0
            place[m, 2 * LANES + lane] = 1.0
        else:
            place[2 * half, lane] = 1.0
    return lax.dot_general(src, jnp.asarray(place), (((0,), (0,)), ((), ())),
                           precision=lax.Precision.HIGHEST)


def _gate_weights(w_r, w_i):
    per_half = LRU_BLOCKS // 2

    def dense(w, lo):
        return jax.scipy.linalg.block_diag(*[w[lo + n] for n in range(per_half)])

    return (0.5 * jnp.stack([jnp.concatenate([dense(w_r, lo), dense(w_i, lo)], axis=1)
                             for lo in (0, per_half)])).astype(BF16)


def kernel(x, mem, positions, g_mix, w_in, da_lq1, da_lk1, da_lq2, da_lk2, da_subln_g,
           lru_conv_w, lru_conv_b, lru_w_r, lru_b_r, lru_w_i, lru_b_i, lru_a_param,
           lru_norm_g, w_out, g_xq, g_mem, xa_wq, xa_wk, xa_wv, xa_wo,
           g_mlp, mlp_w1, mlp_w2, g_final):
    B, S, D = x.shape
    depth = w_in.shape[0]
    T = B * S
    x2 = x.reshape(T, D)
    mem2 = mem.reshape(B * N_MEM, D)
    rope = _rope_lane_tables(positions)
    vec = lambda a: a.reshape(1, -1)

    for l in range(depth):
        lam_init = 0.8 - 0.6 * float(np.exp(-0.3 * l))
        q, k, v, xb, gb = _in_proj(x2, vec(g_mix[l]), w_in[l].astype(BF16), rope)
        o = _attention(q, k, v, vec(da_lq1[l]), vec(da_lk1[l]), vec(da_lq2[l]),
                       vec(da_lk2[l]), vec(da_subln_g[l]), lam_init, B, S)
        wg = jnp.stack([_gate_weights(lru_w_r[l, d], lru_w_i[l, d]) for d in range(2)])
        bg = 0.5 * jnp.concatenate([lru_b_r[l].reshape(2, 1, LRU_WIDTH),
                                    lru_b_i[l].reshape(2, 1, LRU_WIDTH)], axis=-1)
        r = _lru(xb, gb, lru_conv_w[l], vec(lru_conv_b[l]), wg, bg,
                 lru_a_param[l].reshape(2, 1, LRU_WIDTH), vec(lru_norm_g[l]), B, S)
        kx, vx = _mem_kv(mem2, vec(g_mem[l]), xa_wk[l].astype(BF16), xa_wv[l].astype(BF16), B)
        w_o = w_out[l].astype(BF16)
        x2 = _mix_xattn(x2, o, r, w_o[:DA_WIDTH], w_o[DA_WIDTH:], vec(g_xq[l]),
                        xa_wq[l].astype(BF16), kx, vx, xa_wo[l].astype(BF16), S)
        x2 = _mlp(x2, vec(g_mlp[l]), mlp_w1[l].astype(BF16), mlp_w2[l].astype(BF16),
                  vec(g_final), l == depth - 1)
    return x2.reshape(B, S, D)
```

```python
import functools
import math

import numpy as np
import jax
import jax.numpy as jnp
from jax import lax
from jax.experimental import pallas as pl
from jax.experimental.pallas import tpu as pltpu

F32 = jnp.float32
BF16 = jnp.bfloat16

D_MODEL = 1024
N_MEM = 256
DA_HEADS = 4
DA_HEAD_DIM = 64
DA_V_DIM = 128
DA_WIDTH = 512
LRU_WIDTH = 512
LRU_BLOCKS = 8
LRU_BLOCK = 64
LRU_C = 8.0
ROPE_THETA = 500000.0
ROPE_DIM = 16
X_HEADS = 4
X_HEAD_DIM = 256
D_FF = 4096
EPS = 1e-6

LANES = 128
SUBLANES = 8
MIB = 1024 * 1024

ROW_TILE = 512
ATTN_Q_TILE = 1024
ATTN_KEY_CHUNK = 512
ATTN_ROW_STREAMS = 8
LRU_CHUNK = 512
MIX_ROW_TILE = 1024
LRU_UNROLL = 8
LRU_HALO = 16
LRU_DEINTERLEAVE_PIECE = 256
MLP_FF_CHUNK = 1024


def _rms(x, g):
    return x * lax.rsqrt(jnp.mean(x * x, axis=-1, keepdims=True) + EPS) * g


def _const_spec(shape):
    nd = len(shape)
    return pl.BlockSpec(shape, lambda *_: (0,) * nd, pipeline_mode=pl.Buffered(1))


def _params(sem, vmem_mib):
    return pltpu.CompilerParams(dimension_semantics=sem, vmem_limit_bytes=vmem_mib * MIB)


def _in_proj_kernel(x_ref, g_ref, w_ref, rope_ref, q_ref, k_ref, v_ref, xb_ref, gb_ref):
    h = _rms(x_ref[...], g_ref[...]).astype(BF16)
    rc = rope_ref[:, 0:LANES]
    rs = rope_ref[:, LANES:2 * LANES]
    lane = lax.broadcasted_iota(jnp.int32, rc.shape, 1)
    first_half = jnp.bitwise_and(lane, DA_HEAD_DIM - 1) < ROPE_DIM // 2

    def rope(t):
        partner = jnp.where(first_half, pltpu.roll(t, LANES - ROPE_DIM // 2, 1),
                            pltpu.roll(t, ROPE_DIM // 2, 1))
        return t * rc + partner * rs

    zq = jnp.dot(h, w_ref[:, 0:DA_WIDTH], preferred_element_type=F32)
    zk = jnp.dot(h, w_ref[:, DA_WIDTH:2 * DA_WIDTH], preferred_element_type=F32)
    scale = DA_HEAD_DIM ** -0.5 * math.log2(math.e)
    for j in range(DA_WIDTH // LANES):
        sl = slice(j * LANES, (j + 1) * LANES)
        q_ref[:, sl] = (rope(zq[:, sl]) * scale).astype(BF16)
        k_ref[:, sl] = rope(zk[:, sl]).astype(BF16)
    v_ref[...] = jnp.dot(h, w_ref[:, 2 * DA_WIDTH:3 * DA_WIDTH],
                         preferred_element_type=F32).astype(BF16)
    xb_ref[...] = jnp.dot(h, w_ref[:, 3 * DA_WIDTH:3 * DA_WIDTH + LRU_WIDTH],
                          preferred_element_type=F32).astype(BF16)
    gb_ref[...] = jnp.dot(h, w_ref[:, 3 * DA_WIDTH + LRU_WIDTH:],
                          preferred_element_type=F32)


def _in_proj(x2, g, w, rope):
    T = x2.shape[0]
    tm = ROW_TILE
    row = lambda n: pl.BlockSpec((tm, n), lambda i: (i, 0))
    return pl.pallas_call(
        _in_proj_kernel,
        grid=(T // tm,),
        in_specs=[row(D_MODEL), _const_spec((1, D_MODEL)), _const_spec(w.shape),
                  row(2 * LANES)],
        out_specs=[row(DA_WIDTH)] * 5,
        out_shape=[jax.ShapeDtypeStruct((T, DA_WIDTH), BF16)] * 3
        + [jax.ShapeDtypeStruct((T, LRU_WIDTH), BF16), jax.ShapeDtypeStruct((T, LRU_WIDTH), F32)],
        compiler_params=_params(("parallel",), 40),
        name="in_proj",
    )(x2, g, w, rope)


def _attn_kernel(lq1_ref, lk1_ref, lq2_ref, lk2_ref, g_ref, q_ref, k_ref, v_ref, o_ref,
                 vext_ref, *, lam_init):
    @pl.when(pl.program_id(2) == 0)
    def _():
        vext_ref[:, 0:DA_V_DIM] = v_ref[...]
        vext_ref[:, DA_V_DIM:] = jnp.ones(v_ref.shape, BF16)

    lam = (jnp.exp(jnp.sum(lq1_ref[...] * lk1_ref[...], axis=-1, keepdims=True))
           - jnp.exp(jnp.sum(lq2_ref[...] * lk2_ref[...], axis=-1, keepdims=True))
           + lam_init)
    nt = (((1,), (1,)), ((), ()))
    chunks = [slice(c, c + ATTN_KEY_CHUNK) for c in range(0, k_ref.shape[0], ATTN_KEY_CHUNK)]
    tr = q_ref.shape[0] // ATTN_ROW_STREAMS
    lane = lax.broadcasted_iota(jnp.int32, (tr, LANES), 1)
    zero = jnp.zeros((tr, LANES), BF16)

    def softmax_pv(qm):
        ss = [lax.dot_general(qm, k_ref[ks, :], nt, preferred_element_type=F32)
              for ks in chunks]
        m = ss[0]
        for s in ss[1:]:
            m = jnp.maximum(m, s)
        m = jnp.max(m, axis=-1, keepdims=True)
        acc = None
        for s, ks in zip(ss, chunks):
            e = jnp.exp2(s - m).astype(BF16)
            d = jnp.dot(e, vext_ref[ks, :], preferred_element_type=F32)
            acc = d if acc is None else acc + d
        return acc[:, 0:DA_V_DIM] * (1.0 / acc[:, DA_V_DIM:])

    for r in range(ATTN_ROW_STREAMS):
        rows = slice(r * tr, (r + 1) * tr)
        q = q_ref[rows, :]
        o = (softmax_pv(jnp.where(lane < DA_HEAD_DIM, q, zero))
             - lam * softmax_pv(jnp.where(lane >= DA_HEAD_DIM, q, zero)))
        o_ref[rows, :] = (_rms(o, g_ref[...]) * (1.0 - lam_init)).astype(BF16)


def _attention(q, k, v, lq1, lk1, lq2, lk2, g, lam_init, B, S):
    T = q.shape[0]
    tq = ATTN_Q_TILE
    nq = S // tq
    vec = _const_spec((1, DA_HEAD_DIM))
    qspec = pl.BlockSpec((tq, LANES), lambda b, h, i: (b * nq + i, h))
    kvspec = pl.BlockSpec((S, LANES), lambda b, h, i: (b, h))
    return pl.pallas_call(
        functools.partial(_attn_kernel, lam_init=lam_init),
        grid=(B, DA_HEADS, nq),
        in_specs=[vec, vec, vec, vec, _const_spec((1, DA_V_DIM)), qspec, kvspec, kvspec],
        out_specs=qspec,
        out_shape=jax.ShapeDtypeStruct((T, DA_WIDTH), BF16),
        scratch_shapes=[pltpu.VMEM((S, 2 * DA_V_DIM), BF16)],
        compiler_params=_params(("arbitrary", "arbitrary", "arbitrary"), 56),
        name="diff_attn",
    )(lq1, lk1, lq2, lk2, g, q, k, v)


def _interleave_matrix(tc):
    seg_len = tc // SUBLANES
    m = np.zeros((tc, tc), np.float32)
    for t in range(tc):
        m[(t % seg_len) * SUBLANES + t // seg_len, t] = 1.0
    return m


def _lru_kernel(xp_ref, xc_ref, xn_ref, gb_ref, cw_ref, cb_ref, wg_ref, bg_ref, ap_ref,
                ng_ref, il_ref, dil_ref, o_ref, c_ref, a_ref, u_ref, h_ref, pc_ref, hf_ref,
                carry_ref, *, nc, tc):
    p = pl.program_id(1)
    j = pl.program_id(2)
    chunk = j + p * (nc - 1 - 2 * j)
    rows = pl.ds(pl.multiple_of(chunk * tc, SUBLANES), tc)
    ngroups = tc // SUBLANES
    row = lax.broadcasted_iota(jnp.int32, (SUBLANES, LRU_WIDTH), 0)

    @pl.when(p == 0)
    def _():
        x = jnp.dot(il_ref[...], xc_ref[...], preferred_element_type=F32)
        prev = jnp.where(chunk == 0, 0.0, xp_ref[...].astype(F32))
        nxt = jnp.where(chunk == nc - 1, 0.0, xn_ref[...].astype(F32))

        def from_prev_segment(g, first):
            return jnp.where(row == 0, jnp.broadcast_to(first, g.shape), pltpu.roll(g, 1, 0))

        def from_next_segment(g, last):
            return jnp.where(row == SUBLANES - 1, jnp.broadcast_to(last, g.shape),
                             pltpu.roll(g, SUBLANES - 1, 0))

        xe = jnp.concatenate(
            [from_prev_segment(x[tc - 2 * SUBLANES:tc - SUBLANES], prev[LRU_HALO - 2:LRU_HALO - 1]),
             from_prev_segment(x[tc - SUBLANES:tc], prev[LRU_HALO - 1:LRU_HALO]),
             x,
             from_next_segment(x[0:SUBLANES], nxt[0:1])], axis=0)
        cw = cw_ref[...]
        c_ref[rows, :] = (cb_ref[...] + xe[0:tc] * cw[0:1, :]
                          + xe[SUBLANES:tc + SUBLANES] * cw[1:2, :]
                          + xe[2 * SUBLANES:tc + 2 * SUBLANES] * cw[2:3, :]
                          + xe[3 * SUBLANES:tc + 3 * SUBLANES] * cw[3:4, :])

    half = LRU_WIDTH // 2
    c = c_ref[rows, :]
    c16 = c.astype(BF16)
    g_lo = jnp.dot(c16[:, :half], wg_ref[0, 0], preferred_element_type=F32)
    g_hi = jnp.dot(c16[:, half:], wg_ref[0, 1], preferred_element_type=F32)
    bg = bg_ref[0]
    tr = jnp.tanh(jnp.concatenate([g_lo[:, :half], g_hi[:, :half]], axis=1) + bg[:, :LRU_WIDTH])
    ti = jnp.tanh(jnp.concatenate([g_lo[:, half:], g_hi[:, half:]], axis=1) + bg[:, LRU_WIDTH:])
    nap = -ap_ref[0]
    softplus = jnp.maximum(nap, 0.0) + jnp.log1p(jnp.exp(-jnp.abs(nap)))
    kh = softplus * (-0.5 * LRU_C * math.log2(math.e))
    a = jnp.exp2(tr * kh + kh)
    a_ref[...] = a
    y = jnp.maximum(1.0 - a * a, 1e-37)
    u_ref[...] = (c * (y * lax.rsqrt(y))) * (0.5 * ti + 0.5)

    end_prev = jnp.where(j == 0, 0.0, carry_ref[...])

    def scan(reverse):
        def body(g, carry):
            hcur, pcur = carry
            st = pl.multiple_of(((ngroups - 1 - g) if reverse else g) * SUBLANES, SUBLANES)
            av = a_ref[pl.ds(st, SUBLANES), :]
            hcur = av * hcur + u_ref[pl.ds(st, SUBLANES), :]
            pcur = av * pcur
            h_ref[pl.ds(st, SUBLANES), :] = hcur
            pc_ref[pl.ds(st, SUBLANES), :] = pcur
            return hcur, pcur

        start = (jnp.zeros((SUBLANES, LRU_WIDTH), F32), jnp.ones((SUBLANES, LRU_WIDTH), F32))
        hend, pend = lax.fori_loop(0, ngroups, body, start, unroll=LRU_UNROLL)
        shift = SUBLANES - 1 if reverse else 1
        first = (row == SUBLANES - 1) if reverse else (row == 0)
        cin = pltpu.roll(end_prev, shift, 0)
        for _ in range(SUBLANES - 1):
            cin = jnp.where(first, cin, pltpu.roll(hend + pend * cin, shift, 0))
        carry_ref[...] = hend + pend * cin
        return cin

    def states(cin, sl):
        n = (sl.stop - sl.start) // SUBLANES
        local = h_ref[sl, :].reshape(n, SUBLANES, LRU_WIDTH)
        weight = pc_ref[sl, :].reshape(n, SUBLANES, LRU_WIDTH)
        return (local + weight * cin[None]).reshape(n * SUBLANES, LRU_WIDTH)

    @pl.when(p == 0)
    def _():
        hf_ref[rows, :] = states(scan(False), slice(0, tc))

    @pl.when(p == 1)
    def _():
        cin = scan(True)
        hsum = None
        for k0 in range(0, tc, LRU_DEINTERLEAVE_PIECE):
            sl = slice(k0, k0 + LRU_DEINTERLEAVE_PIECE)
            hf = hf_ref[pl.ds(pl.multiple_of(chunk * tc + k0, SUBLANES), LRU_DEINTERLEAVE_PIECE), :]
            d = jnp.dot(dil_ref[:, sl], (states(cin, sl) + hf).astype(BF16),
                        preferred_element_type=F32)
            hsum = d if hsum is None else hsum + d
        gate = gb_ref[...]
        cdf = 0.5 * (1.0 + jnp.tanh(math.sqrt(2.0 / math.pi)
                                    * (gate + 0.044715 * (gate * gate * gate))))
        o_ref[...] = _rms(hsum * (gate * cdf), ng_ref[...]).astype(BF16)


def _lru(xb, gb, cw, cb, wg, bg, ap, ng, B, S):
    T = xb.shape[0]
    tc = LRU_CHUNK
    nc = S // tc
    hb = tc // LRU_HALO
    nhalo = T // LRU_HALO
    il = jnp.asarray(_interleave_matrix(tc), BF16)

    def x_chunk(p, j):
        return j + p * (nc - 1 - j)

    cur = pl.BlockSpec((tc, LRU_WIDTH), lambda b, p, j: (b * nc + x_chunk(p, j), 0))
    prev = pl.BlockSpec((LRU_HALO, LRU_WIDTH),
                        lambda b, p, j: (jnp.maximum((b * nc + x_chunk(p, j)) * hb - 1, 0), 0))
    nxt = pl.BlockSpec((LRU_HALO, LRU_WIDTH),
                       lambda b, p, j: (jnp.minimum((b * nc + x_chunk(p, j) + 1) * hb,
                                                    nhalo - 1), 0))
    out = pl.BlockSpec((tc, LRU_WIDTH), lambda b, p, j: (b * nc + nc - 1 - p * j, 0))
    chunk_scratch = pltpu.VMEM((tc, LRU_WIDTH), F32)
    seq_scratch = pltpu.VMEM((S, LRU_WIDTH), F32)
    return pl.pallas_call(
        functools.partial(_lru_kernel, nc=nc, tc=tc),
        grid=(B, 2, nc),
        in_specs=[prev, cur, nxt, out,
                  _const_spec((4, LRU_WIDTH)), _const_spec((1, LRU_WIDTH)),
                  pl.BlockSpec((1, 2, LRU_WIDTH // 2, LRU_WIDTH), lambda b, p, j: (p, 0, 0, 0)),
                  pl.BlockSpec((1, 1, 2 * LRU_WIDTH), lambda b, p, j: (p, 0, 0)),
                  pl.BlockSpec((1, 1, LRU_WIDTH), lambda b, p, j: (p, 0, 0)),
                  _const_spec((1, LRU_WIDTH)), _const_spec((tc, tc)), _const_spec((tc, tc))],
        out_specs=out,
        out_shape=jax.ShapeDtypeStruct((T, LRU_WIDTH), BF16),
        scratch_shapes=[seq_scratch, chunk_scratch, chunk_scratch, chunk_scratch, chunk_scratch,
                        seq_scratch, pltpu.VMEM((SUBLANES, LRU_WIDTH), F32)],
        compiler_params=_params(("parallel", "arbitrary", "arbitrary"), 48),
        name="rg_lru",
    )(xb, xb, xb, gb, cw, cb, wg, bg, ap, ng, il, il.T)


def _mem_kv_kernel(m_ref, g_ref, wk_ref, wv_ref, k_ref, v_ref):
    m = _rms(m_ref[...], g_ref[...]).astype(BF16)
    k_ref[...] = jnp.dot(m, wk_ref[...], preferred_element_type=F32).astype(BF16)
    v_ref[...] = jnp.dot(m, wv_ref[...], preferred_element_type=F32).astype(BF16)


def _mem_kv(mem2, g, wk, wv, B):
    blk = pl.BlockSpec((N_MEM, D_MODEL), lambda b: (b, 0))
    w = _const_spec((D_MODEL, D_MODEL))
    return pl.pallas_call(
        _mem_kv_kernel,
        grid=(B,),
        in_specs=[blk, _const_spec((1, D_MODEL)), w, w],
        out_specs=[blk, blk],
        out_shape=[jax.ShapeDtypeStruct(mem2.shape, BF16)] * 2,
        compiler_params=_params(("parallel",), 24),
        name="mem_kv",
    )(mem2, g, wk, wv)


def _mix_xattn_kernel(x_ref, o_ref, r_ref, w1_ref, w2_ref, gq_ref, wq_ref, kx_ref, vx_ref,
                      wo_ref, out_ref):
    nt = (((1,), (1,)), ((), ()))
    x = (x_ref[...]
         + jnp.dot(o_ref[...], w1_ref[...], preferred_element_type=F32)
         + jnp.dot(r_ref[...], w2_ref[...], preferred_element_type=F32))
    hq = _rms(x, gq_ref[...]).astype(BF16)
    qx = (jnp.dot(hq, wq_ref[...], preferred_element_type=F32)
          * (X_HEAD_DIM ** -0.5 * math.log2(math.e))).astype(BF16)
    heads = []
    for h in range(X_HEADS):
        sl = slice(h * X_HEAD_DIM, (h + 1) * X_HEAD_DIM)
        s = lax.dot_general(qx[:, sl], kx_ref[:, sl], nt, preferred_element_type=F32)
        e = jnp.exp2(s - jnp.max(s, axis=-1, keepdims=True))
        pr = (e * (1.0 / jnp.sum(e, axis=-1, keepdims=True))).astype(BF16)
        heads.append(jnp.dot(pr, vx_ref[:, sl], preferred_element_type=F32).astype(BF16))
    ox = jnp.concatenate(heads, axis=1)
    out_ref[...] = x + jnp.dot(ox, wo_ref[...], preferred_element_type=F32)


def _mix_xattn(x2, o, r, w_out1, w_out2, gq, wq, kx, vx, wo, S):
    T = x2.shape[0]
    tm = MIX_ROW_TILE
    per_batch = S // tm
    row = lambda n: pl.BlockSpec((tm, n), lambda i: (i, 0))
    memspec = pl.BlockSpec((N_MEM, D_MODEL), lambda i: (i // per_batch, 0))
    w = _const_spec((D_MODEL, D_MODEL))
    wh = _const_spec((DA_WIDTH, D_MODEL))
    return pl.pallas_call(
        _mix_xattn_kernel,
        grid=(T // tm,),
        in_specs=[row(D_MODEL), row(DA_WIDTH), row(LRU_WIDTH), wh, wh,
                  _const_spec((1, D_MODEL)), w, memspec, memspec, w],
        out_specs=row(D_MODEL),
        out_shape=jax.ShapeDtypeStruct((T, D_MODEL), F32),
        compiler_params=_params(("parallel",), 40),
        name="mix_xattn",
    )(x2, o, r, w_out1, w_out2, gq, wq, kx, vx, wo)


def _mlp_kernel(x_ref, g_ref, w1_ref, w2_ref, gf_ref, out_ref, *, final):
    x = x_ref[...]
    hm = _rms(x, g_ref[...]).astype(BF16)
    acc = None
    for c in range(D_FF // MLP_FF_CHUNK):
        sl = slice(c * MLP_FF_CHUNK, (c + 1) * MLP_FF_CHUNK)
        a = jnp.dot(hm, w1_ref[:, sl], preferred_element_type=F32)
        a = jnp.square(jnp.maximum(a, 0.0)).astype(BF16)
        d = jnp.dot(a, w2_ref[sl, :], preferred_element_type=F32)
        acc = d if acc is None else acc + d
    acc = x + acc
    if final:
        acc = _rms(acc, gf_ref[...])
    out_ref[...] = acc


def _mlp(x2, g, w1, w2, gf, final):
    T = x2.shape[0]
    tm = ROW_TILE
    row = pl.BlockSpec((tm, D_MODEL), lambda i: (i, 0))
    vec = _const_spec((1, D_MODEL))
    return pl.pallas_call(
        functools.partial(_mlp_kernel, final=final),
        grid=(T // tm,),
        in_specs=[row, vec, _const_spec(w1.shape), _const_spec(w2.shape), vec],
        out_specs=row,
        out_shape=jax.ShapeDtypeStruct((T, D_MODEL), F32),
        compiler_params=_params(("parallel",), 48),
        name="mlp",
    )(x2, g, w1, w2, gf)


def _rope_lane_tables(positions):
    half = ROPE_DIM // 2
    inv_freq = jnp.power(jnp.float32(ROPE_THETA),
                         -jnp.arange(0, ROPE_DIM, 2, dtype=F32) / ROPE_DIM)
    ang = inv_freq.reshape(-1, 1) * positions.astype(F32).reshape(1, -1)
    src = jnp.concatenate([jnp.cos(ang), jnp.sin(ang), jnp.ones_like(ang[:1])], axis=0)
    place = np.zeros((2 * half + 1, 2 * LANES), np.float32)
    for lane in range(LANES):
        m = lane % DA_HEAD_DIM
        if m < half:
            place[m, lane] = 1.0
            place[half + m, LANES + lane] = -1.0
        elif m < ROPE_DIM:
            place[m - half, lane] = 1.0
            place[m, LANES + lane] = 1.0
        else:
            place[2 * half, lane] = 1.0
    return lax.dot_general(src, jnp.asarray(place), (((0,), (0,)), ((), ())),
                           precision=lax.Precision.HIGHEST)


def _gate_weights(w_r, w_i):
    per_half = LRU_BLOCKS // 2

    def dense(w, lo):
        return jax.scipy.linalg.block_diag(*[w[lo + n] for n in range(per_half)])

    return (0.5 * jnp.stack([jnp.concatenate([dense(w_r, lo), dense(w_i, lo)], axis=1)
                             for lo in (0, per_half)])).astype(BF16)


def kernel(x, mem, positions, g_mix, w_in, da_lq1, da_lk1, da_lq2, da_lk2, da_subln_g,
           lru_conv_w, lru_conv_b, lru_w_r, lru_b_r, lru_w_i, lru_b_i, lru_a_param,
           lru_norm_g, w_out, g_xq, g_mem, xa_wq, xa_wk, xa_wv, xa_wo,
           g_mlp, mlp_w1, mlp_w2, g_final):
    B, S, D = x.shape
    depth = w_in.shape[0]
    T = B * S
    x2 = x.reshape(T, D)
    mem2 = mem.reshape(B * N_MEM, D)
    rope = _rope_lane_tables(positions)
    vec = lambda a: a.reshape(1, -1)

    for l in range(depth):
        lam_init = 0.8 - 0.6 * float(np.exp(-0.3 * l))
        q, k, v, xb, gb = _in_proj(x2, vec(g_mix[l]), w_in[l].astype(BF16), rope)
        o = _attention(q, k, v, vec(da_lq1[l]), vec(da_lk1[l]), vec(da_lq2[l]),
                       vec(da_lk2[l]), vec(da_subln_g[l]), lam_init, B, S)
        wg = jnp.stack([_gate_weights(lru_w_r[l, d], lru_w_i[l, d]) for d in range(2)])
        bg = 0.5 * jnp.concatenate([lru_b_r[l].reshape(2, 1, LRU_WIDTH),
                                    lru_b_i[l].reshape(2, 1, LRU_WIDTH)], axis=-1)
        r = _lru(xb, gb, lru_conv_w[l], vec(lru_conv_b[l]), wg, bg,
                 lru_a_param[l].reshape(2, 1, LRU_WIDTH), vec(lru_norm_g[l]), B, S)
        kx, vx = _mem_kv(mem2, vec(g_mem[l]), xa_wk[l].astype(BF16), xa_wv[l].astype(BF16), B)
        w_o = w_out[l].astype(BF16)
        x2 = _mix_xattn(x2, o, r, w_o[:DA_WIDTH], w_o[DA_WIDTH:], vec(g_xq[l]),
                        xa_wq[l].astype(BF16), kx, vx, xa_wo[l].astype(BF16), S)
        x2 = _mlp(x2, vec(g_mlp[l]), mlp_w1[l].astype(BF16), mlp_w2[l].astype(BF16),
                  vec(g_final), l == depth - 1)
    return x2.reshape(B, S, D)
```
